```python
import math
import jax, jax.numpy as jnp
from jax import lax
import numpy as np

D_MODEL = 1024
BATCH = 8
SEQ = 8192
DEPTH = 4
DEC_BATCH = 4
DEC_SEQ = 4096
PAST_LEN = 128

D_GROUP = 512
N_MIXERS = 3
D_MIX = N_MIXERS * D_GROUP
D_FF = 2816
EPS = 1e-6
CONV_W = 4
CONV_LEFT = CONV_W // 2
CONV_RIGHT = CONV_W - 1 - CONV_LEFT
LRU_BLOCKS = 8
LRU_BLOCK = D_GROUP // LRU_BLOCKS
LRU_C = 8.0
SSD_HEADS = 8
SSD_HEAD_DIM = D_GROUP // SSD_HEADS
SSD_STATE = 64
SSD_GROUPS = 2
SSD_HPG = SSD_HEADS // SSD_GROUPS
SSD_CHUNK = 128
SSD_XBC = D_GROUP + 2 * SSD_GROUPS * SSD_STATE
RET_HEADS = 4
RET_HEAD_DIM = D_GROUP // RET_HEADS
RET_CHUNK = 128
ROPE_BASE = 10000.0
PROJ_SIZES = (D_GROUP, D_GROUP, D_GROUP, SSD_XBC, 2 * SSD_HEADS, D_GROUP, D_GROUP, D_GROUP, D_GROUP)
D_PROJ = 7 * D_GROUP + SSD_XBC + 2 * SSD_HEADS

kernel_name = 'hybrid_bidir_rglru_ssd_retention_encoder'


def _rmsnorm(x, w):
    xf = x.astype(jnp.float32)
    xf = xf * lax.rsqrt(jnp.mean(xf * xf, axis=-1, keepdims=True) + EPS)
    return (xf * w.astype(jnp.float32)).astype(x.dtype)


def _swiglu(x, w_gu, w_down):
    g, u = jnp.split(x @ w_gu, 2, axis=-1)
    return (jax.nn.silu(g) * u) @ w_down


def _centred_dwconv(x, w, b):
    s = x.shape[1]
    xp = jnp.pad(x, ((0, 0), (CONV_LEFT, CONV_RIGHT), (0, 0)))
    out = b
    for tap in range(CONV_W):
        out = out + xp[:, tap:tap + s] * w[tap]
    return out


def _linear_scan(a, b, reverse):
    def combine(e1, e2):
        a1, b1 = e1
        a2, b2 = e2
        return a1 * a2, a2 * b1 + b2
    return lax.associative_scan(combine, (a, b), reverse=reverse, axis=1)[1]


def _rglru_group(xb, gate, conv_w, conv_b, w_a, b_a, w_i, b_i, lam):
    f32 = jnp.float32
    bsz, s, _ = xb.shape
    xc = _centred_dwconv(xb, conv_w, conv_b).astype(f32)
    xblk = xc.reshape(bsz, s, LRU_BLOCKS, LRU_BLOCK)
    h_sum = jnp.zeros_like(xc)
    for d, rev in ((0, False), (1, True)):
        r = jax.nn.sigmoid(jnp.einsum('bsni,nij->bsnj', xblk, w_a[d].astype(f32)).reshape(bsz, s, D_GROUP) + b_a[d].astype(f32))
        i = jax.nn.sigmoid(jnp.einsum('bsni,nij->bsnj', xblk, w_i[d].astype(f32)).reshape(bsz, s, D_GROUP) + b_i[d].astype(f32))
        log_a = -LRU_C * r * jax.nn.softplus(-lam[d].astype(f32))
        u = jnp.sqrt(-jnp.expm1(2.0 * log_a)) * (i * xc)
        h_sum = h_sum + _linear_scan(jnp.exp(log_a), u, rev)
    return h_sum * jax.nn.gelu(gate.astype(f32))


def _ssd_chunked(x, dt, a, bm, cm):
    bsz, s, h, p = x.shape
    L = SSD_CHUNK
    nc = s // L
    x = x.reshape(bsz, nc, L, SSD_GROUPS, SSD_HPG, p)
    dt = dt.reshape(bsz, nc, L, SSD_GROUPS, SSD_HPG)
    bm = bm.reshape(bsz, nc, L, SSD_GROUPS, SSD_STATE)
    cm = cm.reshape(bsz, nc, L, SSD_GROUPS, SSD_STATE)
    acum = jnp.cumsum(dt * a.reshape(SSD_GROUPS, SSD_HPG), axis=2)
    acum_t = jnp.moveaxis(acum, 2, -1)
    dt_t = jnp.moveaxis(dt, 2, -1)
    tril = jnp.tril(jnp.ones((L, L), dtype=bool))
    seg = acum_t[..., :, None] - acum_t[..., None, :]
    decay = jnp.exp(jnp.where(tril, seg, -jnp.inf))
    cb = jnp.einsum('bcign,bcjgn->bcgij', cm, bm)
    w = cb[:, :, :, None] * decay * dt_t[..., None, :]
    y_diag = jnp.einsum('bcgkij,bcjgkp->bcigkp', w, x)
    decay_states = jnp.exp(acum_t[..., -1:] - acum_t) * dt_t
    states = jnp.einsum('bcgkl,bclgn,bclgkp->bcgkpn', decay_states, bm, x)
    chunk_decay = jnp.exp(acum_t[..., -1])

    def step(carry, inp):
        st, ad = inp
        return carry * ad[..., None, None] + st, carry

    init = jnp.zeros((bsz, SSD_GROUPS, SSD_HPG, p, SSD_STATE), x.dtype)
    _, prev = lax.scan(step, init, (jnp.moveaxis(states, 1, 0), jnp.moveaxis(chunk_decay, 1, 0)))
    prev = jnp.moveaxis(prev, 0, 1)
    y_off = jnp.einsum('bclgn,bcgkpn,bcgkl->bclgkp', cm, prev, jnp.exp(acum_t))
    return (y_diag + y_off).reshape(bsz, s, h, p)


def _ssd_group(z, xbc, dt_raw, conv_w, conv_b, dt_bias, a_log, d_skip, norm_w):
    f32 = jnp.float32
    bsz, s, _ = xbc.shape
    nbc = SSD_GROUPS * SSD_STATE
    xbc = jax.nn.silu(_centred_dwconv(xbc, conv_w, conv_b).astype(f32))
    xs = xbc[..., :D_GROUP].reshape(bsz, s, SSD_HEADS, SSD_HEAD_DIM)
    bm = xbc[..., D_GROUP:D_GROUP + nbc].reshape(bsz, s, SSD_GROUPS, SSD_STATE)
    cm = xbc[..., D_GROUP + nbc:].reshape(bsz, s, SSD_GROUPS, SSD_STATE)
    dt = jax.nn.softplus(dt_raw.astype(f32).reshape(bsz, s, 2, SSD_HEADS) + dt_bias.astype(f32))
    a = -jnp.exp(a_log.astype(f32))
    flip = lambda t: jnp.flip(t, axis=1)
    y_f = _ssd_chunked(xs, dt[:, :, 0], a[0], bm, cm)
    y_b = flip(_ssd_chunked(flip(xs), flip(dt[:, :, 1]), a[1], flip(bm), flip(cm)))
    y = (y_f + y_b + d_skip.astype(f32)[:, None] * xs).reshape(bsz, s, D_GROUP)
    y = y * jax.nn.silu(z.astype(f32))
    y = y * lax.rsqrt(jnp.mean(y * y, axis=-1, keepdims=True) + EPS)
    return y * norm_w.astype(f32)


def _rope(t):
    s, d = t.shape[1], t.shape[-1]
    inv_freq = 1.0 / (ROPE_BASE ** (jnp.arange(0, d, 2, dtype=jnp.float32) / d))
    ang = jnp.arange(s, dtype=jnp.float32)[:, None] * inv_freq[None, :]
    cos = jnp.cos(ang)[None, :, None, :]
    sin = jnp.sin(ang)[None, :, None, :]
    t1, t2 = jnp.split(t, 2, axis=-1)
    return jnp.concatenate([t1 * cos - t2 * sin, t1 * sin + t2 * cos], axis=-1)


def _retention_group(q, k, v, g, norm_w):
    f32 = jnp.float32
    bsz, s, _ = q.shape
    L = RET_CHUNK
    nc = s // L
    shp = (bsz, s, RET_HEADS, RET_HEAD_DIM)
    cshp = (bsz, nc, L, RET_HEADS, RET_HEAD_DIM)
    q = _rope(q.astype(f32).reshape(shp)).reshape(cshp)
    k = (_rope(k.astype(f32).reshape(shp)) * RET_HEAD_DIM ** -0.5).reshape(cshp)
    v = v.astype(f32).reshape(cshp)
    log_gamma = jnp.log1p(-jnp.exp2(-5.0 - jnp.arange(RET_HEADS, dtype=f32)))
    pos = jnp.arange(L, dtype=f32)
    d_intra = jnp.exp(log_gamma[:, None, None] * jnp.abs(pos[:, None] - pos[None, :]))
    scores = jnp.einsum('bcihd,bcjhd->bchij', q, k) * d_intra
    y = jnp.einsum('bchij,bcjhe->bcihe', scores, v)
    kv_f = jnp.einsum('bclhd,bclhe,hl->bchde', k, v, jnp.exp(log_gamma[:, None] * (L - 1.0 - pos)[None]))
    kv_b = jnp.einsum('bclhd,bclhe,hl->bchde', k, v, jnp.exp(log_gamma[:, None] * pos[None]))
    cdec = jnp.exp(log_gamma * L)[:, None, None]

    def step(carry, kv):
        return carry * cdec + kv, carry

    init = jnp.zeros((bsz, RET_HEADS, RET_HEAD_DIM, RET_HEAD_DIM), f32)
    _, r_f = lax.scan(step, init, jnp.moveaxis(kv_f, 1, 0))
    _, r_b = lax.scan(step, init, jnp.moveaxis(kv_b, 1, 0), reverse=True)
    r_f = jnp.moveaxis(r_f, 0, 1)
    r_b = jnp.moveaxis(r_b, 0, 1)
    y = y + jnp.einsum('bclhd,bchde,hl->bclhe', q, r_f, jnp.exp(log_gamma[:, None] * (pos + 1.0)[None]))
    y = y + jnp.einsum('bclhd,bchde,hl->bclhe', q, r_b, jnp.exp(log_gamma[:, None] * (L - pos)[None]))
    y = y.reshape(shp)
    mu = jnp.mean(y, axis=-1, keepdims=True)
    var = jnp.mean(jnp.square(y - mu), axis=-1, keepdims=True)
    y = ((y - mu) * lax.rsqrt(var + EPS)).reshape(bsz, s, D_GROUP) * norm_w.astype(f32)
    return y * jax.nn.silu(g.astype(f32))


def _mixer(xn, l, p):
    offsets = [int(o) for o in np.cumsum(PROJ_SIZES)[:-1]]
    (lru_x, lru_gate, ssd_z, ssd_xbc, ssd_dt, ret_q, ret_k, ret_v, ret_g) = jnp.split(xn @ p['w_in'][l], offsets, axis=-1)
    y_lru = _rglru_group(lru_x, lru_gate, p['lru_conv_w'][l], p['lru_conv_b'][l], p['lru_w_a'][l], p['lru_b_a'][l], p['lru_w_i'][l], p['lru_b_i'][l], p['lru_lam'][l])
    y_ssd = _ssd_group(ssd_z, ssd_xbc, ssd_dt, p['ssd_conv_w'][l], p['ssd_conv_b'][l], p['ssd_dt_bias'][l], p['ssd_a_log'][l], p['ssd_d'][l], p['ssd_norm'][l])
    y_ret = _retention_group(ret_q, ret_k, ret_v, ret_g, p['ret_norm'][l])
    y = jnp.concatenate([y_lru, y_ssd, y_ret], axis=-1).astype(xn.dtype)
    return y @ p['w_out'][l]


def _trunk(x, p):
    for l in range(DEPTH):
        x = x + 0.5 * _swiglu(_rmsnorm(x, p['ffn1_norm'][l]), p['ffn1_w_gu'][l], p['ffn1_w_down'][l])
        x = x + _mixer(_rmsnorm(x, p['mix_norm'][l]), l, p)
        x = x + 0.5 * _swiglu(_rmsnorm(x, p['ffn2_norm'][l]), p['ffn2_w_gu'][l], p['ffn2_w_down'][l])
    return _rmsnorm(x, p['final_norm'])


def setup_inputs(seed: int = 0) -> dict:
    key = jax.random.key(seed)
    ks = jax.random.split(key, 26)
    f32 = jnp.float32

    def nrm(k, shape, scale):
        return jax.random.normal(k, shape, f32) * scale

    def gain(k, shape):
        return 1.0 + 0.02 * jax.random.normal(k, shape, f32)

    u = jax.random.uniform(ks[13], (DEPTH, 2, D_GROUP), f32, 0.9, 0.999)
    a0 = u ** (1.0 / LRU_C)
    dt0 = jnp.exp(jax.random.uniform(ks[16], (DEPTH, 2, SSD_HEADS), f32, math.log(1e-3), math.log(1e-1)))
    return {
        'x_prompt': jax.random.normal(ks[0], (BATCH, SEQ, D_MODEL), f32),
        'x_sample': jax.random.normal(ks[1], (DEC_BATCH, DEC_SEQ, D_MODEL), f32),
        'ffn1_norm': gain(ks[2], (DEPTH, D_MODEL)),
        'ffn1_w_gu': nrm(ks[3], (DEPTH, D_MODEL, 2 * D_FF), D_MODEL ** -0.5),
        'ffn1_w_down': nrm(ks[4], (DEPTH, D_FF, D_MODEL), D_FF ** -0.5),
        'mix_norm': gain(ks[5], (DEPTH, D_MODEL)),
        'w_in': nrm(ks[6], (DEPTH, D_MODEL, D_PROJ), D_MODEL ** -0.5),
        'lru_conv_w': nrm(ks[7], (DEPTH, CONV_W, D_GROUP), CONV_W ** -0.5),
        'lru_conv_b': nrm(ks[8], (DEPTH, D_GROUP), 0.02),
        'lru_w_a': nrm(ks[9], (DEPTH, 2, LRU_BLOCKS, LRU_BLOCK, LRU_BLOCK), LRU_BLOCK ** -0.5),
        'lru_b_a': nrm(ks[10], (DEPTH, 2, D_GROUP), 0.02),
        'lru_w_i': nrm(ks[11], (DEPTH, 2, LRU_BLOCKS, LRU_BLOCK, LRU_BLOCK), LRU_BLOCK ** -0.5),
        'lru_b_i': nrm(ks[12], (DEPTH, 2, D_GROUP), 0.02),
        'lru_lam': jnp.log(a0) - jnp.log1p(-a0),
        'ssd_conv_w': nrm(ks[14], (DEPTH, CONV_W, SSD_XBC), CONV_W ** -0.5),
        'ssd_conv_b': nrm(ks[15], (DEPTH, SSD_XBC), 0.02),
        'ssd_dt_bias': dt0 + jnp.log(-jnp.expm1(-dt0)),
        'ssd_a_log': jnp.log(jax.random.uniform(ks[17], (DEPTH, 2, SSD_HEADS), f32, 1.0, 16.0)),
        'ssd_d': gain(ks[18], (DEPTH, SSD_HEADS)),
        'ssd_norm': gain(ks[19], (DEPTH, D_GROUP)),
        'ret_norm': gain(ks[20], (DEPTH, D_GROUP)),
        'w_out': nrm(ks[21], (DEPTH, D_MIX, D_MODEL), D_MIX ** -0.5),
        'ffn2_norm': gain(ks[22], (DEPTH, D_MODEL)),
        'ffn2_w_gu': nrm(ks[23], (DEPTH, D_MODEL, 2 * D_FF), D_MODEL ** -0.5),
        'ffn2_w_down': nrm(ks[24], (DEPTH, D_FF, D_MODEL), D_FF ** -0.5),
        'final_norm': gain(ks[25], (D_MODEL,)),
    }


def reference(x_prompt, x_sample, ffn1_norm, ffn1_w_gu, ffn1_w_down, mix_norm, w_in,
              lru_conv_w, lru_conv_b, lru_w_a, lru_b_a, lru_w_i, lru_b_i, lru_lam,
              ssd_conv_w, ssd_conv_b, ssd_dt_bias, ssd_a_log, ssd_d, ssd_norm,
              ret_norm, w_out, ffn2_norm, ffn2_w_gu, ffn2_w_down, final_norm):
    p = {
        'ffn1_norm': ffn1_norm, 'ffn1_w_gu': ffn1_w_gu, 'ffn1_w_down': ffn1_w_down,
        'mix_norm': mix_norm, 'w_in': w_in,
        'lru_conv_w': lru_conv_w, 'lru_conv_b': lru_conv_b, 'lru_w_a': lru_w_a, 'lru_b_a': lru_b_a,
        'lru_w_i': lru_w_i, 'lru_b_i': lru_b_i, 'lru_lam': lru_lam,
        'ssd_conv_w': ssd_conv_w, 'ssd_conv_b': ssd_conv_b, 'ssd_dt_bias': ssd_dt_bias,
        'ssd_a_log': ssd_a_log, 'ssd_d': ssd_d, 'ssd_norm': ssd_norm,
        'ret_norm': ret_norm, 'w_out': w_out,
        'ffn2_norm': ffn2_norm, 'ffn2_w_gu': ffn2_w_gu, 'ffn2_w_down': ffn2_w_down,
        'final_norm': final_norm,
    }
    y_prompt = _trunk(x_prompt, p)
    y_sample = _trunk(x_sample, p)
    return (y_prompt, y_sample)
```

```python
import functools
import math

import jax
import jax.numpy as jnp
from jax import lax
from jax.experimental import pallas as pl
from jax.experimental.pallas import tpu as pltpu

F32 = jnp.float32
BF16 = jnp.bfloat16

D_MODEL = 1024
D_FF = 2816
D_GROUP = 512
EPS = 1e-6
CONV_W = 4
CONV_LEFT = 2
LRU_BLOCKS = 8
LRU_BLOCK = D_GROUP // LRU_BLOCKS
LRU_C = 8.0
SSD_HEADS = 8
SSD_HEAD_DIM = 64
SSD_STATE = 64
SSD_GROUPS = 2
SSD_HPG = SSD_HEADS // SSD_GROUPS
SSD_XBC = D_GROUP + 2 * SSD_GROUPS * SSD_STATE
RET_HEADS = 4
RET_HEAD_DIM = 128
CHUNK = 128
ROPE_BASE = 10000.0

LANES = 128
BF16_SUBLANES = 16
HALO = BF16_SUBLANES
VMEM_LIMIT = 56 * 1024 * 1024

N_CONV = D_GROUP + SSD_XBC
N_ACT = N_CONV + 3 * D_GROUP
N_GATE = 3 * D_GROUP
Q_OFF = N_CONV
K_OFF = N_CONV + D_GROUP
V_OFF = N_CONV + 2 * D_GROUP
N_PART = 3 * D_GROUP

FFN_TOKENS = 512
PROJ_TOKENS = 512
MIX_TOKENS = 512
FF_CHUNK = 512
NEG = -1e30

_NT = (((1,), (1,)), ((), ()))
_TN = (((0,), (0,)), ((), ()))


def _dot(a, b):
    return jnp.dot(a, b, preferred_element_type=F32)


def _dot_hi(a, b):
    return jnp.dot(a, b, preferred_element_type=F32, precision=lax.Precision.HIGHEST)


def _rms(x, w):
    return x * lax.rsqrt(jnp.mean(x * x, axis=-1, keepdims=True) + EPS) * w


def _sigmoid(x):
    return 1.0 / (1.0 + jnp.exp(-x))


def _silu(x):
    return x * _sigmoid(x)


def _softplus(x):
    return jnp.maximum(x, 0.0) + jnp.log(1.0 + jnp.exp(-jnp.abs(x)))


def _gelu_tanh(x):
    c = math.sqrt(2.0 / math.pi)
    return 0.5 * x * (1.0 + jnp.tanh(c * (x + 0.044715 * (x * x * x))))


def _ffn_body(x_ref, nw_ref, wgu_ref, wd_ref, o_ref):
    x = x_ref[...]
    xb = _rms(x, nw_ref[...]).astype(BF16)
    acc = jnp.zeros(x.shape, F32)
    for c in range(0, D_FF, FF_CHUNK):
        w = min(FF_CHUNK, D_FF - c)
        g = _dot(xb, wgu_ref[:, c:c + w])
        u = _dot(xb, wgu_ref[:, D_FF + c:D_FF + c + w])
        a = (_silu(g) * u).astype(BF16)
        acc = acc + _dot(a, wd_ref[c:c + w, :])
    o_ref[...] = x + 0.5 * acc


def _ffn(x, nw, wgu, wd):
    t = x.shape[0]
    tm = min(FFN_TOKENS, t)
    const = lambda i: (0, 0)
    return pl.pallas_call(
        _ffn_body,
        out_shape=jax.ShapeDtypeStruct(x.shape, F32),
        grid=(t // tm,),
        in_specs=[
            pl.BlockSpec((tm, D_MODEL), lambda i: (i, 0)),
            pl.BlockSpec((1, D_MODEL), const),
            pl.BlockSpec((D_MODEL, 2 * D_FF), const, pipeline_mode=pl.Buffered(1)),
            pl.BlockSpec((D_FF, D_MODEL), const, pipeline_mode=pl.Buffered(1)),
        ],
        out_specs=pl.BlockSpec((tm, D_MODEL), lambda i: (i, 0)),
        compiler_params=pltpu.CompilerParams(
            dimension_semantics=("arbitrary",), vmem_limit_bytes=VMEM_LIMIT),
        name="ffn",
    )(x, nw, wgu, wd)


def _proj_body(x_ref, nw_ref, wa_ref, wg_ref, wdt_ref, c2_ref, s2_ref, a_ref, g_ref, dt_ref):
    xb = _rms(x_ref[...], nw_ref[...]).astype(BF16)
    pa = _dot(xb, wa_ref[...])
    a_ref[:, 0:Q_OFF] = pa[:, 0:Q_OFF].astype(BF16)
    a_ref[:, V_OFF:N_ACT] = pa[:, V_OFF:N_ACT].astype(BF16)
    c2 = c2_ref[...]
    s2 = s2_ref[...]
    kscale = RET_HEAD_DIM ** -0.5
    for h in range(2 * RET_HEADS):
        lo = Q_OFF + h * RET_HEAD_DIM
        blk = pa[:, lo:lo + RET_HEAD_DIM]
        rot = blk * c2 + pltpu.roll(blk, RET_HEAD_DIM // 2, 1) * s2
        if h >= RET_HEADS:
            rot = rot * kscale
        a_ref[:, lo:lo + RET_HEAD_DIM] = rot.astype(BF16)
    g_ref[...] = _dot(xb, wg_ref[...]).astype(BF16)
    dt_ref[...] = _dot(xb, wdt_ref[...])


def _proj(x, nw, wa, wg, wdt, c2, s2):
    b, s, _ = x.shape
    tm = min(PROJ_TOKENS, s)
    const = lambda bi, i: (0, 0)
    row = lambda bi, i: (bi, i, 0)
    return pl.pallas_call(
        _proj_body,
        out_shape=(jax.ShapeDtypeStruct((b, s, N_ACT), BF16),
                   jax.ShapeDtypeStruct((b, s, N_GATE), BF16),
                   jax.ShapeDtypeStruct((b, s, LANES), F32)),
        grid=(b, s // tm),
        in_specs=[
            pl.BlockSpec((None, tm, D_MODEL), row),
            pl.BlockSpec((1, D_MODEL), const),
            pl.BlockSpec((D_MODEL, N_ACT), const, pipeline_mode=pl.Buffered(1)),
            pl.BlockSpec((D_MODEL, N_GATE), const, pipeline_mode=pl.Buffered(1)),
            pl.BlockSpec((D_MODEL, LANES), const, pipeline_mode=pl.Buffered(1)),
            pl.BlockSpec((tm, LANES), lambda bi, i: (i, 0)),
            pl.BlockSpec((tm, LANES), lambda bi, i: (i, 0)),
        ],
        out_specs=(pl.BlockSpec((None, tm, N_ACT), row),
                   pl.BlockSpec((None, tm, N_GATE), row),
                   pl.BlockSpec((None, tm, LANES), row)),
        compiler_params=pltpu.CompilerParams(
            dimension_semantics=("arbitrary", "arbitrary"), vmem_limit_bytes=VMEM_LIMIT),
        name="mix_proj",
    )(x, nw, wa, wg, wdt, c2, s2)


def _conv(act_ref, prev_ref, next_ref, cw_ref, cb_ref, ext_ref, first, last):
    tb = act_ref.shape[0]
    keep_prev = jnp.where(first, 0.0, 1.0)
    keep_next = jnp.where(last, 0.0, 1.0)
    ext_ref[0:HALO, :] = prev_ref[...].astype(F32) * keep_prev
    ext_ref[HALO:HALO + tb, :] = act_ref[:, 0:N_CONV].astype(F32)
    ext_ref[HALO + tb:HALO + tb + HALO, :] = next_ref[...].astype(F32) * keep_next
    out = cb_ref[...]
    for tap in range(CONV_W):
        lo = HALO - CONV_LEFT + tap
        out = out + ext_ref[lo:lo + tb, :] * cw_ref[tap:tap + 1, :]
    return out


def _lru_inputs(xc, wg_ref, bg_ref, sp_ref):
    gates = _dot(xc.astype(BF16), wg_ref[...]) + bg_ref[...]
    r = _sigmoid(gates[:, 0:D_GROUP])
    ig = _sigmoid(gates[:, D_GROUP:2 * D_GROUP])
    log_a = (-LRU_C) * r * _softplus(-sp_ref[...])
    a = jnp.exp(log_a)
    u = jnp.sqrt(1.0 - a * a) * (ig * xc)
    return a, u


def _scan(a, u, reverse):
    tb = a.shape[0]
    rows = lax.broadcasted_iota(jnp.int32, a.shape, 0)
    s = 1
    while s < tb:
        if reverse:
            a_s = pltpu.roll(a, tb - s, 0)
            u_s = pltpu.roll(u, tb - s, 0)
            m = rows < tb - s
        else:
            a_s = pltpu.roll(a, s, 0)
            u_s = pltpu.roll(u, s, 0)
            m = rows >= s
        u = jnp.where(m, a * u_s + u, u)
        a = jnp.where(m, a * a_s, a)
        s *= 2
    return a, u


def _tri_consts():
    r = lax.broadcasted_iota(jnp.int32, (CHUNK, CHUNK), 0)
    c = lax.broadcasted_iota(jnp.int32, (CHUNK, CHUNK), 1)
    return r, c


def _expand_mat():
    r = lax.broadcasted_iota(jnp.int32, (LANES, D_GROUP), 0)
    c = lax.broadcasted_iota(jnp.int32, (LANES, D_GROUP), 1)
    return jnp.where(r == c // SSD_HEAD_DIM, 1.0, 0.0)


def _ret_gamma_log(h):
    return math.log1p(-(2.0 ** (-5.0 - h)))


def _mixf_body(act_ref, prev_ref, next_ref, dtr_ref, cw_ref, cb_ref, wg_ref, bg_ref, lam_ref,
               dtb_ref, alog_ref, dsk_ref, part_ref,
               ext_ref, xbc_ref, dts_ref, hc_ref, sf_ref, rf_ref):
    i = pl.program_id(1)
    n = pl.num_programs(1)
    tb = act_ref.shape[0]

    @pl.when(i == 0)
    def _():
        hc_ref[...] = jnp.zeros(hc_ref.shape, F32)
        sf_ref[...] = jnp.zeros(sf_ref.shape, F32)
        rf_ref[...] = jnp.zeros(rf_ref.shape, F32)

    conv = _conv(act_ref, prev_ref, next_ref, cw_ref, cb_ref, ext_ref, i == 0, i == n - 1)
    xc = conv[:, 0:D_GROUP]
    xbc_ref[...] = _silu(conv[:, D_GROUP:N_CONV])

    a, u = _lru_inputs(xc, wg_ref, bg_ref, lam_ref)
    acum, hloc = _scan(a, u, reverse=False)
    h = hloc + acum * hc_ref[...]
    hc_ref[...] = h[tb - 1:tb, :]
    part_ref[:, 0:D_GROUP] = h

    dts_ref[...] = _softplus(dtr_ref[...] + dtb_ref[...])

    r_i, c_i = _tri_consts()
    tril = r_i >= c_i
    triu = r_i <= c_i
    ltri = jnp.where(tril, 1.0, 0.0)
    utri = jnp.where(triu, 1.0, 0.0)
    dist = jnp.abs(r_i - c_i).astype(F32)
    expand = _expand_mat()
    expand_b = expand.astype(BF16)
    lane = lax.broadcasted_iota(jnp.int32, (1, LANES), 1)
    a_row = jnp.where(lane < 2 * SSD_HEADS, -jnp.exp(alog_ref[...]), 0.0)
    pos = lax.broadcasted_iota(jnp.int32, (CHUNK, 1), 0).astype(F32)

    def chunk(c, carry):
        rs = pl.ds(pl.multiple_of(c * CHUNK, CHUNK), CHUNK)
        x = xbc_ref[rs, 0:D_GROUP]
        bm = xbc_ref[rs, D_GROUP:D_GROUP + LANES].astype(BF16)
        cm = xbc_ref[rs, D_GROUP + LANES:SSD_XBC].astype(BF16)
        dt = dts_ref[rs, :]
        da = dt * a_row
        cum = _dot_hi(ltri, da)
        rcum = _dot_hi(utri, da)
        da_t = da.T
        cum_t = _dot_hi(da_t[0:BF16_SUBLANES, :], utri)
        rcum_t = _dot_hi(da_t[0:BF16_SUBLANES, :], ltri)
        dt_t = dt.T
        xb = x.astype(BF16)
        cum_last = cum[CHUNK - 1:CHUNK, :]
        ds_f = jnp.exp(cum_last - cum) * dt
        ds_x = _dot(ds_f.astype(BF16), expand_b)
        ecum_x = _dot(jnp.exp(cum).astype(BF16), expand_b)
        cdec_x = _dot_hi(jnp.broadcast_to(jnp.exp(cum_last), (8, LANES)), expand)[0:1, :]
        xd = (x * ds_x).astype(BF16)
        ys = []
        for g in range(SSD_GROUPS):
            gs = slice(g * SSD_STATE, (g + 1) * SSD_STATE)
            cb = lax.dot_general(cm[:, gs], bm[:, gs], _NT, preferred_element_type=F32)
            hs = slice(g * 256, (g + 1) * 256)
            y_off = _dot(cm[:, gs], sf_ref[g].astype(BF16)) * ecum_x[:, hs]
            for k in range(SSD_HPG):
                hd = g * SSD_HPG + k
                seg_f = cum[:, hd:hd + 1] - cum_t[hd:hd + 1, :]
                m_f = jnp.exp(jnp.where(tril, seg_f, NEG)) * dt_t[hd:hd + 1, :]
                hb = SSD_HEADS + hd
                seg_b = rcum[:, hb:hb + 1] - rcum_t[hb:hb + 1, :]
                m_b = jnp.exp(jnp.where(triu, seg_b, NEG)) * dt_t[hb:hb + 1, :]
                w = (cb * (m_f + m_b)).astype(BF16)
                cs = slice(hd * SSD_HEAD_DIM, (hd + 1) * SSD_HEAD_DIM)
                yd = _dot(w, xb[:, cs])
                ys.append(yd + y_off[:, k * SSD_HEAD_DIM:(k + 1) * SSD_HEAD_DIM])
            st = lax.dot_general(bm[:, gs], xd[:, hs], _TN, preferred_element_type=F32)
            sf_ref[g] = sf_ref[g] * cdec_x[:, hs] + st
        part_ref[rs, D_GROUP:2 * D_GROUP] = jnp.concatenate(ys, axis=1) + dsk_ref[...] * x

        yr = []
        for hd in range(RET_HEADS):
            lg = _ret_gamma_log(hd)
            qs = slice(Q_OFF + hd * RET_HEAD_DIM, Q_OFF + (hd + 1) * RET_HEAD_DIM)
            ks = slice(K_OFF + hd * RET_HEAD_DIM, K_OFF + (hd + 1) * RET_HEAD_DIM)
            vs = slice(V_OFF + hd * RET_HEAD_DIM, V_OFF + (hd + 1) * RET_HEAD_DIM)
            q = act_ref[rs, qs]
            k = act_ref[rs, ks]
            v = act_ref[rs, vs]
            sc = lax.dot_general(q, k, _NT, preferred_element_type=F32) * jnp.exp(lg * dist)
            y = _dot(sc.astype(BF16), v)
            y = y + _dot(q, rf_ref[hd].astype(BF16)) * jnp.exp(lg * (pos + 1.0))
            kd = (k.astype(F32) * jnp.exp(lg * (CHUNK - 1.0 - pos))).astype(BF16)
            kv = lax.dot_general(kd, v, _TN, preferred_element_type=F32)
            rf_ref[hd] = rf_ref[hd] * math.exp(lg * CHUNK) + kv
            yr.append(y)
        part_ref[rs, 2 * D_GROUP:3 * D_GROUP] = jnp.concatenate(yr, axis=1)
        return carry

    lax.fori_loop(0, tb // CHUNK, chunk, 0)


def _halo_specs(tb, s, rev):
    nb = s // tb
    per = tb // HALO
    last = s // HALO - 1
    blk = (lambda i: nb - 1 - i) if rev else (lambda i: i)
    prev = pl.BlockSpec((None, HALO, N_CONV), lambda b, i: (b, jnp.maximum(blk(i) * per - 1, 0), 0))
    nxt = pl.BlockSpec((None, HALO, N_CONV), lambda b, i: (b, jnp.minimum((blk(i) + 1) * per, last), 0))
    return blk, prev, nxt


def _mix_fwd(act, dtr, cw, cb, wg, bg, lam, dtb, alog, dsk):
    b, s, _ = act.shape
    tb = min(MIX_TOKENS, s)
    blk, prev, nxt = _halo_specs(tb, s, False)
    const = lambda bi, i: (0, 0)
    row = lambda bi, i: (bi, i, 0)
    return pl.pallas_call(
        _mixf_body,
        out_shape=jax.ShapeDtypeStruct((b, s, N_PART), F32),
        grid=(b, s // tb),
        in_specs=[
            pl.BlockSpec((None, tb, N_ACT), row),
            prev, nxt,
            pl.BlockSpec((None, tb, LANES), row),
            pl.BlockSpec((CONV_W, N_CONV), const),
            pl.BlockSpec((1, N_CONV), const),
            pl.BlockSpec((D_GROUP, 2 * D_GROUP), const),
            pl.BlockSpec((1, 2 * D_GROUP), const),
            pl.BlockSpec((1, D_GROUP), const),
            pl.BlockSpec((1, LANES), const),
            pl.BlockSpec((1, LANES), const),
            pl.BlockSpec((1, D_GROUP), const),
        ],
        out_specs=pl.BlockSpec((None, tb, N_PART), row),
        scratch_shapes=[
            pltpu.VMEM((tb + 2 * HALO, N_CONV), F32),
            pltpu.VMEM((tb, SSD_XBC), F32),
            pltpu.VMEM((tb, LANES), F32),
            pltpu.VMEM((1, D_GROUP), F32),
            pltpu.VMEM((SSD_GROUPS, SSD_STATE, SSD_HPG * SSD_HEAD_DIM), F32),
            pltpu.VMEM((RET_HEADS, RET_HEAD_DIM, RET_HEAD_DIM), F32),
        ],
        compiler_params=pltpu.CompilerParams(
            dimension_semantics=("arbitrary", "arbitrary"), vmem_limit_bytes=VMEM_LIMIT),
        name="mix_fwd",
    )(act, act, act, dtr, cw, cb, wg, bg, lam, dtb, alog, dsk)


def _mixb_body(x_ref, act_ref, prev_ref, next_ref, gate_ref, dtr_ref, part_ref,
               cw_ref, cb_ref, wg_ref, bg_ref, lam_ref, dtb_ref, alog_ref,
               snw_ref, rnw_ref, wo_ref, o_ref,
               ext_ref, xbc_ref, dts_ref, y_ref, hc_ref, sb_ref, rb_ref):
    i = pl.program_id(1)
    n = pl.num_programs(1)
    tb = act_ref.shape[0]

    @pl.when(i == 0)
    def _():
        hc_ref[...] = jnp.zeros(hc_ref.shape, F32)
        sb_ref[...] = jnp.zeros(sb_ref.shape, F32)
        rb_ref[...] = jnp.zeros(rb_ref.shape, F32)

    conv = _conv(act_ref, prev_ref, next_ref, cw_ref, cb_ref, ext_ref, i == n - 1, i == 0)
    xc = conv[:, 0:D_GROUP]
    xbc_ref[...] = _silu(conv[:, D_GROUP:N_CONV])

    a, u = _lru_inputs(xc, wg_ref, bg_ref, lam_ref)
    acum, hloc = _scan(a, u, reverse=True)
    h = hloc + acum * hc_ref[...]
    hc_ref[...] = h[0:1, :]
    gate = gate_ref[:, 0:D_GROUP].astype(F32)
    y_ref[:, 0:D_GROUP] = (part_ref[:, 0:D_GROUP] + h) * _gelu_tanh(gate)

    dts_ref[...] = _softplus(dtr_ref[...] + dtb_ref[...])

    r_i, c_i = _tri_consts()
    utri = jnp.where(r_i <= c_i, 1.0, 0.0)
    expand = _expand_mat()
    expand_b = expand.astype(BF16)
    lane = lax.broadcasted_iota(jnp.int32, (1, LANES), 1)
    a_row = jnp.where(lane < 2 * SSD_HEADS, -jnp.exp(alog_ref[...]), 0.0)
    pos = lax.broadcasted_iota(jnp.int32, (CHUNK, 1), 0).astype(F32)
    nck = tb // CHUNK

    def chunk(j, carry):
        c = nck - 1 - j
        rs = pl.ds(pl.multiple_of(c * CHUNK, CHUNK), CHUNK)
        x = xbc_ref[rs, 0:D_GROUP]
        bm = xbc_ref[rs, D_GROUP:D_GROUP + LANES].astype(BF16)
        cm = xbc_ref[rs, D_GROUP + LANES:SSD_XBC].astype(BF16)
        dt = dts_ref[rs, :]
        da = dt * a_row
        rcum = pltpu.roll(_dot_hi(utri, da), LANES - SSD_HEADS, 1)
        dtb = pltpu.roll(dt, LANES - SSD_HEADS, 1)
        rcum_first = rcum[0:1, :]
        ds_b = jnp.exp(rcum_first - rcum) * dtb
        ds_x = _dot(ds_b.astype(BF16), expand_b)
        ecum_x = _dot(jnp.exp(rcum).astype(BF16), expand_b)
        cdec_x = _dot_hi(jnp.broadcast_to(jnp.exp(rcum_first), (8, LANES)), expand)[0:1, :]
        xd = (x * ds_x).astype(BF16)
        yo = []
        for g in range(SSD_GROUPS):
            gs = slice(g * SSD_STATE, (g + 1) * SSD_STATE)
            hs = slice(g * 256, (g + 1) * 256)
            yo.append(_dot(cm[:, gs], sb_ref[g].astype(BF16)) * ecum_x[:, hs])
            st = lax.dot_general(bm[:, gs], xd[:, hs], _TN, preferred_element_type=F32)
            sb_ref[g] = sb_ref[g] * cdec_x[:, hs] + st
        y_ref[rs, D_GROUP:2 * D_GROUP] = part_ref[rs, D_GROUP:2 * D_GROUP] + jnp.concatenate(yo, axis=1)

        yr = []
        for hd in range(RET_HEADS):
            lg = _ret_gamma_log(hd)
            qs = slice(Q_OFF + hd * RET_HEAD_DIM, Q_OFF + (hd + 1) * RET_HEAD_DIM)
            ks = slice(K_OFF + hd * RET_HEAD_DIM, K_OFF + (hd + 1) * RET_HEAD_DIM)
            vs = slice(V_OFF + hd * RET_HEAD_DIM, V_OFF + (hd + 1) * RET_HEAD_DIM)
            q = act_ref[rs, qs]
            k = act_ref[rs, ks]
            v = act_ref[rs, vs]
            yr.append(_dot(q, rb_ref[hd].astype(BF16)) * jnp.exp(lg * (CHUNK - pos)))
            kd = (k.astype(F32) * jnp.exp(lg * pos)).astype(BF16)
            kv = lax.dot_general(kd, v, _TN, preferred_element_type=F32)
            rb_ref[hd] = rb_ref[hd] * math.exp(lg * CHUNK) + kv
        y_ref[rs, 2 * D_GROUP:3 * D_GROUP] = part_ref[rs, 2 * D_GROUP:3 * D_GROUP] + jnp.concatenate(yr, axis=1)
        return carry

    lax.fori_loop(0, nck, chunk, 0)

    ys = y_ref[:, D_GROUP:2 * D_GROUP] * _silu(gate_ref[:, D_GROUP:2 * D_GROUP].astype(F32))
    ys = ys * lax.rsqrt(jnp.mean(ys * ys, axis=-1, keepdims=True) + EPS) * snw_ref[...]
    y_ref[:, D_GROUP:2 * D_GROUP] = ys
    for hd in range(RET_HEADS):
        cs = slice(2 * D_GROUP + hd * RET_HEAD_DIM, 2 * D_GROUP + (hd + 1) * RET_HEAD_DIM)
        ws = slice(hd * RET_HEAD_DIM, (hd + 1) * RET_HEAD_DIM)
        y = y_ref[:, cs]
        mu = jnp.mean(y, axis=-1, keepdims=True)
        yc = y - mu
        var = jnp.mean(yc * yc, axis=-1, keepdims=True)
        gs = slice(2 * D_GROUP + hd * RET_HEAD_DIM, 2 * D_GROUP + (hd + 1) * RET_HEAD_DIM)
        y_ref[:, cs] = yc * lax.rsqrt(var + EPS) * rnw_ref[:, ws] * _silu(gate_ref[:, gs].astype(F32))
    o_ref[...] = x_ref[...] + _dot(y_ref[...].astype(BF16), wo_ref[...])


def _mix_bwd(x, act, gate, dtr, part, cw, cb, wg, bg, lam, dtb, alog, snw, rnw, wo):
    b, s, _ = act.shape
    tb = min(MIX_TOKENS, s)
    blk, prev, nxt = _halo_specs(tb, s, True)
    const = lambda bi, i: (0, 0)
    row = lambda bi, i: (bi, blk(i), 0)
    return pl.pallas_call(
        _mixb_body,
        out_shape=jax.ShapeDtypeStruct((b, s, D_MODEL), F32),
        grid=(b, s // tb),
        in_specs=[
            pl.BlockSpec((None, tb, D_MODEL), row),
            pl.BlockSpec((None, tb, N_ACT), row),
            prev, nxt,
            pl.BlockSpec((None, tb, N_GATE), row),
            pl.BlockSpec((None, tb, LANES), row),
            pl.BlockSpec((None, tb, N_PART), row),
            pl.BlockSpec((CONV_W, N_CONV), const),
            pl.BlockSpec((1, N_CONV), const),
            pl.BlockSpec((D_GROUP, 2 * D_GROUP), const),
            pl.BlockSpec((1, 2 * D_GROUP), const),
            pl.BlockSpec((1, D_GROUP), const),
            pl.BlockSpec((1, LANES), const),
            pl.BlockSpec((1, LANES), const),
            pl.BlockSpec((1, D_GROUP), const),
            pl.BlockSpec((1, D_GROUP), const),
            pl.BlockSpec((N_PART, D_MODEL), const),
        ],
        out_specs=pl.BlockSpec((None, tb, D_MODEL), row),
        scratch_shapes=[
            pltpu.VMEM((tb + 2 * HALO, N_CONV), F32),
            pltpu.VMEM((tb, SSD_XBC), F32),
            pltpu.VMEM((tb, LANES), F32),
            pltpu.VMEM((tb, N_PART), F32),
            pltpu.VMEM((1, D_GROUP), F32),
            pltpu.VMEM((SSD_GROUPS, SSD_STATE, SSD_HPG * SSD_HEAD_DIM), F32),
            pltpu.VMEM((RET_HEADS, RET_HEAD_DIM, RET_HEAD_DIM), F32),
        ],
        compiler_params=pltpu.CompilerParams(
            dimension_semantics=("arbitrary", "arbitrary"), vmem_limit_bytes=VMEM_LIMIT),
        name="mix_bwd",
    )(x, act, act, act, gate, dtr, part, cw, cb, wg, bg, lam, dtb, alog, snw, rnw, wo)


def _norm_body(x_ref, w_ref, o_ref):
    o_ref[...] = _rms(x_ref[...], w_ref[...])


def _final_norm(x, w):
    t = x.shape[0]
    tm = min(1024, t)
    return pl.pallas_call(
        _norm_body,
        out_shape=jax.ShapeDtypeStruct(x.shape, F32),
        grid=(t // tm,),
        in_specs=[pl.BlockSpec((tm, D_MODEL), lambda i: (i, 0)),
                  pl.BlockSpec((1, D_MODEL), lambda i: (0, 0))],
        out_specs=pl.BlockSpec((tm, D_MODEL), lambda i: (i, 0)),
        compiler_params=pltpu.CompilerParams(dimension_semantics=("arbitrary",)),
        name="final_norm",
    )(x, w)


def _block_diag(w):
    nb, bw, _ = w.shape
    eye = jnp.eye(nb, dtype=w.dtype)
    return (eye[:, None, :, None] * w[:, :, None, :]).reshape(nb * bw, nb * bw)


def _pad_lanes(v):
    return jnp.pad(v, (0, LANES - v.shape[0]))[None, :]


def _prep_layer(l, p):
    w_in = p['w_in'][l]
    o = [0, 512, 1024, 1536, 2304, 2320, 2832, 3344, 3856, 4368]
    lru_x, lru_gate, ssd_z, xbc, dtc, q, k, v, g = [w_in[:, o[j]:o[j + 1]] for j in range(9)]
    gate_w = []
    gate_b = []
    for d in range(2):
        gate_w.append(jnp.concatenate([_block_diag(p['lru_w_a'][l, d]), _block_diag(p['lru_w_i'][l, d])], axis=1))
        gate_b.append(jnp.concatenate([p['lru_b_a'][l, d], p['lru_b_i'][l, d]])[None, :])
    return dict(
        ffn1_nw=p['ffn1_norm'][l][None, :], ffn1_wgu=p['ffn1_w_gu'][l].astype(BF16), ffn1_wd=p['ffn1_w_down'][l].astype(BF16),
        ffn2_nw=p['ffn2_norm'][l][None, :], ffn2_wgu=p['ffn2_w_gu'][l].astype(BF16), ffn2_wd=p['ffn2_w_down'][l].astype(BF16),
        mix_nw=p['mix_norm'][l][None, :],
        wa=jnp.concatenate([lru_x, xbc, q, k, v], axis=1).astype(BF16),
        wg=jnp.concatenate([lru_gate, ssd_z, g], axis=1).astype(BF16),
        wdt=jnp.pad(dtc, ((0, 0), (0, LANES - dtc.shape[1]))).astype(BF16),
        cw=jnp.concatenate([p['lru_conv_w'][l], p['ssd_conv_w'][l]], axis=1),
        cb=jnp.concatenate([p['lru_conv_b'][l], p['ssd_conv_b'][l]])[None, :],
        gate_w=[w.astype(BF16) for w in gate_w], gate_b=gate_b,
        lam=[p['lru_lam'][l, d][None, :] for d in range(2)],
        dtb=_pad_lanes(p['ssd_dt_bias'][l].reshape(-1)),
        alog=_pad_lanes(p['ssd_a_log'][l].reshape(-1)),
        dsk=jnp.repeat(p['ssd_d'][l], SSD_HEAD_DIM)[None, :],
        snw=p['ssd_norm'][l][None, :], rnw=p['ret_norm'][l][None, :],
        wo=p['w_out'][l].astype(BF16),
    )


def _rope_tables(s):
    d = RET_HEAD_DIM
    inv_freq = 1.0 / (ROPE_BASE ** (jnp.arange(0, d, 2, dtype=F32) / d))
    ang = jnp.arange(s, dtype=F32)[:, None] * inv_freq[None, :]
    cos = jnp.cos(ang)
    sin = jnp.sin(ang)
    return jnp.concatenate([cos, cos], axis=1), jnp.concatenate([-sin, sin], axis=1)


def _trunk(x, layers, final_nw):
    b, s, d = x.shape
    c2, s2 = _rope_tables(s)
    for lp in layers:
        x = _ffn(x.reshape(b * s, d), lp['ffn1_nw'], lp['ffn1_wgu'], lp['ffn1_wd']).reshape(b, s, d)
        act, gate, dtr = _proj(x, lp['mix_nw'], lp['wa'], lp['wg'], lp['wdt'], c2, s2)
        part = _mix_fwd(act, dtr, lp['cw'], lp['cb'], lp['gate_w'][0], lp['gate_b'][0], lp['lam'][0],
                        lp['dtb'], lp['alog'], lp['dsk'])
        x = _mix_bwd(x, act, gate, dtr, part, lp['cw'], lp['cb'], lp['gate_w'][1], lp['gate_b'][1], lp['lam'][1],
                     lp['dtb'], lp['alog'], lp['snw'], lp['rnw'], lp['wo'])
        x = _ffn(x.reshape(b * s, d), lp['ffn2_nw'], lp['ffn2_wgu'], lp['ffn2_wd']).reshape(b, s, d)
    return _final_norm(x.reshape(b * s, d), final_nw).reshape(b, s, d)


def kernel(x_prompt, x_sample, ffn1_norm, ffn1_w_gu, ffn1_w_down, mix_norm, w_in, lru_conv_w, lru_conv_b, lru_w_a, lru_b_a, lru_w_i, lru_b_i, lru_lam, ssd_conv_w, ssd_conv_b, ssd_dt_bias, ssd_a_log, ssd_d, ssd_norm, ret_norm, w_out, ffn2_norm, ffn2_w_gu, ffn2_w_down, final_norm):
    p = dict(ffn1_norm=ffn1_norm, ffn1_w_gu=ffn1_w_gu, ffn1_w_down=ffn1_w_down, mix_norm=mix_norm, w_in=w_in,
             lru_conv_w=lru_conv_w, lru_conv_b=lru_conv_b, lru_w_a=lru_w_a, lru_b_a=lru_b_a, lru_w_i=lru_w_i,
             lru_b_i=lru_b_i, lru_lam=lru_lam, ssd_conv_w=ssd_conv_w, ssd_conv_b=ssd_conv_b, ssd_dt_bias=ssd_dt_bias,
             ssd_a_log=ssd_a_log, ssd_d=ssd_d, ssd_norm=ssd_norm, ret_norm=ret_norm, w_out=w_out,
             ffn2_norm=ffn2_norm, ffn2_w_gu=ffn2_w_gu, ffn2_w_down=ffn2_w_down)
    layers = [_prep_layer(l, p) for l in range(ffn1_norm.shape[0])]
    final_nw = final_norm[None, :]
    return (_trunk(x_prompt, layers, final_nw), _trunk(x_sample, layers, final_nw))
```

```python
import math

import jax
import jax.numpy as jnp
from jax import lax
from jax.experimental import pallas as pl
from jax.experimental.pallas import tpu as pltpu

F32 = jnp.float32
BF16 = jnp.bfloat16

D_MODEL = 1024
D_FF = 2816
D_GROUP = 512
EPS = 1e-6
CONV_W = 4
LRU_C = 8.0
SSD_HEADS = 8
SSD_HEAD_DIM = 64
SSD_STATE = 64
SSD_GROUPS = 2
SSD_HPG = SSD_HEADS // SSD_GROUPS
SSD_XBC = D_GROUP + 2 * SSD_GROUPS * SSD_STATE
RET_HEADS = 4
RET_HEAD_DIM = 128
CHUNK = 128
ROPE_BASE = 10000.0

LANES = 128
SUBLANES = 8
BF16_SUBLANES = 16
SEG = CHUNK // SUBLANES
HALO = BF16_SUBLANES
VMEM_LIMIT = 56 * 1024 * 1024

N_CONV = D_GROUP + SSD_XBC
N_ACT = N_CONV + 3 * D_GROUP
N_GATE = 3 * D_GROUP
Q_OFF = N_CONV
K_OFF = N_CONV + D_GROUP
V_OFF = N_CONV + 2 * D_GROUP
N_PART = 3 * D_GROUP

FFN_TOKENS = 512
PROJ_TOKENS = 512
MIX_TOKENS = 512
FF_CHUNK = 512
TINY = 1e-30

_NT = (((1,), (1,)), ((), ()))
_TN = (((0,), (0,)), ((), ()))


def _dot(a, b):
    return jnp.dot(a, b, preferred_element_type=F32)


def _dot_hi(a, b):
    return jnp.dot(a, b, preferred_element_type=F32, precision=lax.Precision.HIGHEST)


def _rms(x, w):
    return x * lax.rsqrt(jnp.mean(x * x, axis=-1, keepdims=True) + EPS) * w


def _sigmoid(x):
    return 1.0 / (1.0 + jnp.exp(-x))


def _silu(x):
    return x * _sigmoid(x)


def _softplus(x):
    return jnp.maximum(x, 0.0) + jnp.log(1.0 + jnp.exp(-jnp.abs(x)))


def _gelu_tanh(x):
    c = math.sqrt(2.0 / math.pi)
    return 0.5 * x * (1.0 + jnp.tanh(c * (x + 0.044715 * (x * x * x))))


def _ffn_body(x_ref, nw_ref, wgu_ref, wd_ref, o_ref):
    x = x_ref[...].reshape(-1, D_MODEL)
    xb = _rms(x, nw_ref[...]).astype(BF16)
    acc = jnp.zeros(x.shape, F32)
    for c in range(0, D_FF, FF_CHUNK):
        w = min(FF_CHUNK, D_FF - c)
        g = _dot(xb, wgu_ref[:, c:c + w])
        u = _dot(xb, wgu_ref[:, D_FF + c:D_FF + c + w])
        a = (_silu(g) * u).astype(BF16)
        acc = acc + _dot(a, wd_ref[c:c + w, :])
    o_ref[...] = (x + 0.5 * acc).reshape(o_ref.shape)


def _ffn_weight_specs():
    const = lambda *_: (0, 0)
    return [
        pl.BlockSpec((1, D_MODEL), const),
        pl.BlockSpec((D_MODEL, 2 * D_FF), const, pipeline_mode=pl.Buffered(1)),
        pl.BlockSpec((D_FF, D_MODEL), const, pipeline_mode=pl.Buffered(1)),
    ]


def _ffn(x, nw, wgu, wd):
    t = x.shape[0]
    tm = min(FFN_TOKENS, t)
    return pl.pallas_call(
        _ffn_body,
        out_shape=jax.ShapeDtypeStruct(x.shape, F32),
        grid=(t // tm,),
        in_specs=[pl.BlockSpec((tm, D_MODEL), lambda i: (i, 0))] + _ffn_weight_specs(),
        out_specs=pl.BlockSpec((tm, D_MODEL), lambda i: (i, 0)),
        compiler_params=pltpu.CompilerParams(
            dimension_semantics=("arbitrary",), vmem_limit_bytes=VMEM_LIMIT),
        name="ffn",
    )(x, nw, wgu, wd)


def _proj_body(x_ref, nw_ref, wa_ref, wg_ref, wdt_ref, c2_ref, s2_ref, a_ref, g_ref, dt_ref):
    xb = _rms(x_ref[...], nw_ref[...]).astype(BF16)
    pa = _dot(xb, wa_ref[...])
    a_ref[:, 0:Q_OFF] = pa[:, 0:Q_OFF].astype(BF16)
    a_ref[:, V_OFF:N_ACT] = pa[:, V_OFF:N_ACT].astype(BF16)
    c2 = c2_ref[...]
    s2 = s2_ref[...]
    kscale = RET_HEAD_DIM ** -0.5
    for h in range(2 * RET_HEADS):
        lo = Q_OFF + h * RET_HEAD_DIM
        blk = pa[:, lo:lo + RET_HEAD_DIM]
        rot = blk * c2 + pltpu.roll(blk, RET_HEAD_DIM // 2, 1) * s2
        if h >= RET_HEADS:
            rot = rot * kscale
        a_ref[:, lo:lo + RET_HEAD_DIM] = rot.astype(BF16)
    g_ref[...] = _dot(xb, wg_ref[...]).astype(BF16)
    dt_ref[...] = _dot(xb, wdt_ref[...])


def _proj(x, nw, wa, wg, wdt, c2, s2):
    b, s, _ = x.shape
    tm = min(PROJ_TOKENS, s)
    const = lambda bi, i: (0, 0)
    row = lambda bi, i: (bi, i, 0)
    return pl.pallas_call(
        _proj_body,
        out_shape=(jax.ShapeDtypeStruct((b, s, N_ACT), BF16),
                   jax.ShapeDtypeStruct((b, s, N_GATE), BF16),
                   jax.ShapeDtypeStruct((b, s, LANES), F32)),
        grid=(b, s // tm),
        in_specs=[
            pl.BlockSpec((None, tm, D_MODEL), row),
            pl.BlockSpec((1, D_MODEL), const),
            pl.BlockSpec((D_MODEL, N_ACT), const, pipeline_mode=pl.Buffered(1)),
            pl.BlockSpec((D_MODEL, N_GATE), const, pipeline_mode=pl.Buffered(1)),
            pl.BlockSpec((D_MODEL, LANES), const, pipeline_mode=pl.Buffered(1)),
            pl.BlockSpec((tm, LANES), lambda bi, i: (i, 0)),
            pl.BlockSpec((tm, LANES), lambda bi, i: (i, 0)),
        ],
        out_specs=(pl.BlockSpec((None, tm, N_ACT), row),
                   pl.BlockSpec((None, tm, N_GATE), row),
                   pl.BlockSpec((None, tm, LANES), row)),
        compiler_params=pltpu.CompilerParams(
            dimension_semantics=("arbitrary", "arbitrary"), vmem_limit_bytes=VMEM_LIMIT),
        name="mix_proj",
    )(x, nw, wa, wg, wdt, c2, s2)


def _split(x):
    return x.reshape(x.shape[0] // CHUNK, SEG, SUBLANES, x.shape[1])


def _seg_rows(x4, j):
    return x4[:, j].reshape(-1, x4.shape[3])


def _conv(x, prev, nxt, cw_ref, cb_ref):
    x4 = _split(x)
    nck, _, _, ch = x4.shape
    nseg = nck * SUBLANES
    rows = lax.broadcasted_iota(jnp.int32, (nseg, ch), 0)
    n1 = jnp.where(rows == nseg - 1, nxt[0:1, :], pltpu.roll(_seg_rows(x4, 0), nseg - 1, 0))
    p1 = jnp.where(rows == 0, prev[HALO - 1:HALO, :], pltpu.roll(_seg_rows(x4, SEG - 1), 1, 0))
    p2 = jnp.where(rows == 0, prev[HALO - 1 - SUBLANES:HALO - SUBLANES, :],
                   pltpu.roll(_seg_rows(x4, SEG - 2), 1, 0))
    as4 = lambda v: v.reshape(nck, 1, SUBLANES, ch)
    xp1 = jnp.concatenate([x4[:, 1:SEG], as4(n1)], axis=1)
    xm1 = jnp.concatenate([as4(p1), x4[:, 0:SEG - 1]], axis=1)
    xm2 = jnp.concatenate([as4(p2), as4(p1), x4[:, 0:SEG - 2]], axis=1)
    w = cw_ref[...]
    out = cb_ref[...] + xm2 * w[0:1, :] + xm1 * w[1:2, :] + x4 * w[2:3, :] + xp1 * w[3:4, :]
    return out.reshape(x.shape)


def _lru_inputs(xc, wg_ref, bg_ref, lam_ref):
    gates = _dot(xc.astype(BF16), wg_ref[...]) + bg_ref[...]
    r = _sigmoid(gates[:, 0:D_GROUP])
    ig = _sigmoid(gates[:, D_GROUP:2 * D_GROUP])
    log_a = (-LRU_C) * r * _softplus(-lam_ref[...])
    a = jnp.exp(log_a)
    om = 1.0 - a * a
    u = om * lax.rsqrt(jnp.maximum(om, TINY)) * (ig * xc)
    return a, u


def _row_scan(a, u, reverse):
    n = a.shape[0]
    rows = lax.broadcasted_iota(jnp.int32, a.shape, 0)
    s = 1
    while s < n:
        if reverse:
            a_s = pltpu.roll(a, n - s, 0)
            u_s = pltpu.roll(u, n - s, 0)
            m = rows < n - s
        else:
            a_s = pltpu.roll(a, s, 0)
            u_s = pltpu.roll(u, s, 0)
            m = rows >= s
        u = jnp.where(m, a * u_s + u, u)
        a = jnp.where(m, a * a_s, a)
        s *= 2
    return a, u


def _lru_scan(a, u, carry, reverse):
    a4 = _split(a)
    u4 = _split(u)
    nck, _, _, ch = a4.shape
    nseg = nck * SUBLANES
    order = range(SEG - 1, -1, -1) if reverse else range(SEG)
    acc_a = {}
    acc_u = {}
    pa = pu = None
    for j in order:
        aj = a4[:, j]
        uj = u4[:, j]
        if pa is not None:
            uj = aj * pu + uj
            aj = aj * pa
        acc_a[j], acc_u[j] = aj, uj
        pa, pu = aj, uj
    tot_a, hend = _row_scan(pa.reshape(nseg, ch), pu.reshape(nseg, ch), reverse)
    hend = hend + tot_a * carry
    rows = lax.broadcasted_iota(jnp.int32, (nseg, ch), 0)
    if reverse:
        cin = jnp.where(rows == nseg - 1, carry, pltpu.roll(hend, nseg - 1, 0))
        new_carry = hend[0:1, :]
    else:
        cin = jnp.where(rows == 0, carry, pltpu.roll(hend, 1, 0))
        new_carry = hend[nseg - 1:nseg, :]
    cin = cin.reshape(nck, SUBLANES, ch)
    h = jnp.stack([acc_u[j] + acc_a[j] * cin for j in range(SEG)], axis=1)
    return h.reshape(a.shape), new_carry


def _time_consts():
    r = lax.broadcasted_iota(jnp.int32, (CHUNK, CHUNK), 0)
    c = lax.broadcasted_iota(jnp.int32, (CHUNK, CHUNK), 1)
    tr = (r % SUBLANES) * SEG + r // SUBLANES
    tc = (c % SUBLANES) * SEG + c // SUBLANES
    return tr, tc


def _expand_mat(first_row):
    r = lax.broadcasted_iota(jnp.int32, (LANES, D_GROUP), 0)
    c = lax.broadcasted_iota(jnp.int32, (LANES, D_GROUP), 1)
    return jnp.where(r == first_row + c // SSD_HEAD_DIM, 1.0, 0.0)


def _ret_gamma_log(h):
    return math.log1p(-(2.0 ** (-5.0 - h)))


def _neg_exp_alog(alog_ref):
    lane = lax.broadcasted_iota(jnp.int32, (1, LANES), 1)
    return jnp.where(lane < 2 * SSD_HEADS, -jnp.exp(alog_ref[...]), 0.0)


def _mixf_body(act_ref, prev_ref, next_ref, dtr_ref, cw_ref, cb_ref, wg_ref, bg_ref, lam_ref,
               dtb_ref, alog_ref, dsk_ref, part_ref, cv_ref,
               xbc_ref, dts_ref, hc_ref, sf_ref, rf_ref):
    i = pl.program_id(1)
    n = pl.num_programs(1)
    tb = act_ref.shape[0]

    @pl.when(i == 0)
    def _():
        hc_ref[...] = jnp.zeros(hc_ref.shape, F32)
        sf_ref[...] = jnp.zeros(sf_ref.shape, F32)
        rf_ref[...] = jnp.zeros(rf_ref.shape, F32)

    prev = prev_ref[...].astype(F32) * jnp.where(i == 0, 0.0, 1.0)
    nxt = next_ref[...].astype(F32) * jnp.where(i == n - 1, 0.0, 1.0)
    conv = _conv(act_ref[:, 0:N_CONV].astype(F32), prev, nxt, cw_ref, cb_ref)
    xc = conv[:, 0:D_GROUP]
    xbc = _silu(conv[:, D_GROUP:N_CONV])
    xbc_ref[...] = xbc
    cv_ref[:, 0:D_GROUP] = xc.astype(BF16)
    cv_ref[:, D_GROUP:N_CONV] = xbc.astype(BF16)

    a, u = _lru_inputs(xc, wg_ref, bg_ref, lam_ref)
    h, hc_ref[...] = _lru_scan(a, u, hc_ref[...], reverse=False)
    part_ref[:, 0:D_GROUP] = h

    dts_ref[...] = _softplus(dtr_ref[...] + dtb_ref[...])

    tr, tc = _time_consts()
    tril = tr >= tc
    ltri = jnp.where(tril, 1.0, 0.0)
    utri = jnp.where(tr <= tc, 1.0, 0.0)
    eye = jnp.where(tr == tc, 1.0, 0.0)
    dist = jnp.abs(tr - tc).astype(F32)
    pos = tr[:, 0:1].astype(F32)
    expand = _expand_mat(0)
    expand_b = expand.astype(BF16)
    a_row = _neg_exp_alog(alog_ref)
    last = CHUNK - 1
    ret_d = [jnp.exp(_ret_gamma_log(hd) * dist) for hd in range(RET_HEADS)]
    ret_q = [jnp.exp(_ret_gamma_log(hd) * (pos + 1.0)) for hd in range(RET_HEADS)]
    ret_k = [jnp.exp(_ret_gamma_log(hd) * (CHUNK - 1.0 - pos)) for hd in range(RET_HEADS)]

    def chunk(c, carry):
        rs = pl.ds(pl.multiple_of(c * CHUNK, CHUNK), CHUNK)
        x = xbc_ref[rs, 0:D_GROUP]
        bm = xbc_ref[rs, D_GROUP:D_GROUP + LANES].astype(BF16)
        cm = xbc_ref[rs, D_GROUP + LANES:SSD_XBC].astype(BF16)
        dt = dts_ref[rs, :]
        da = dt * a_row
        cum = _dot_hi(ltri, da)
        tot = cum[last:last + 1, :]
        rcum = tot - cum + da
        da_t = da.T[0:BF16_SUBLANES, :]
        dt_t = dt.T[0:BF16_SUBLANES, :]
        cum_t = _dot_hi(da_t, utri)
        rcum_t = cum_t[:, last:last + 1] - cum_t + da_t
        xb = x.astype(BF16)
        e2 = _dot(jnp.concatenate([jnp.exp(tot - cum) * dt, jnp.exp(cum)], axis=0).astype(BF16), expand_b)
        xd = (x * e2[0:CHUNK, :]).astype(BF16)
        ecum_x = e2[CHUNK:2 * CHUNK, :]
        cdec_x = _dot_hi(jnp.broadcast_to(jnp.exp(tot), (SUBLANES, LANES)), expand)[0:1, :]
        ys = []
        for g in range(SSD_GROUPS):
            gs = slice(g * SSD_STATE, (g + 1) * SSD_STATE)
            hs = slice(g * 256, (g + 1) * 256)
            cb = lax.dot_general(cm[:, gs], bm[:, gs], _NT, preferred_element_type=F32)
            y_off = _dot(cm[:, gs], sf_ref[g].astype(BF16)) * ecum_x[:, hs]
            for k in range(SSD_HPG):
                hd = g * SSD_HPG + k
                hb = SSD_HEADS + hd
                z = jnp.where(tril, cum[:, hd:hd + 1] - cum_t[hd:hd + 1, :],
                              rcum[:, hb:hb + 1] - rcum_t[hb:hb + 1, :])
                d = jnp.where(tril, dt_t[hd:hd + 1, :], dt_t[hb:hb + 1, :])
                m = jnp.exp(z) * d + eye * dt_t[hb:hb + 1, :]
                w = (cb * m).astype(BF16)
                cs = slice(hd * SSD_HEAD_DIM, (hd + 1) * SSD_HEAD_DIM)
                ys.append(_dot(w, xb[:, cs]) + y_off[:, k * SSD_HEAD_DIM:(k + 1) * SSD_HEAD_DIM])
            st = lax.dot_general(bm[:, gs], xd[:, hs], _TN, preferred_element_type=F32)
            sf_ref[g] = sf_ref[g] * cdec_x[:, hs] + st
        part_ref[rs, D_GROUP:2 * D_GROUP] = jnp.concatenate(ys, axis=1) + dsk_ref[...] * x

        yr = []
        for hd in range(RET_HEADS):
            q = act_ref[rs, Q_OFF + hd * RET_HEAD_DIM:Q_OFF + (hd + 1) * RET_HEAD_DIM]
            k = act_ref[rs, K_OFF + hd * RET_HEAD_DIM:K_OFF + (hd + 1) * RET_HEAD_DIM]
            v = act_ref[rs, V_OFF + hd * RET_HEAD_DIM:V_OFF + (hd + 1) * RET_HEAD_DIM]
            sc = lax.dot_general(q, k, _NT, preferred_element_type=F32) * ret_d[hd]
            y = _dot(sc.astype(BF16), v) + _dot(q, rf_ref[hd].astype(BF16)) * ret_q[hd]
            kd = (k.astype(F32) * ret_k[hd]).astype(BF16)
            kv = lax.dot_general(kd, v, _TN, preferred_element_type=F32)
            rf_ref[hd] = rf_ref[hd] * math.exp(_ret_gamma_log(hd) * CHUNK) + kv
            yr.append(y)
        part_ref[rs, 2 * D_GROUP:3 * D_GROUP] = jnp.concatenate(yr, axis=1)
        return carry

    lax.fori_loop(0, tb // CHUNK, chunk, 0)


def _mix_fwd(act, dtr, cw, cb, wg, bg, lam, dtb, alog, dsk):
    b, s, _ = act.shape
    tb = min(MIX_TOKENS, s)
    per = tb // HALO
    last = s // HALO - 1
    const = lambda bi, i: (0, 0)
    row = lambda bi, i: (bi, i, 0)
    return pl.pallas_call(
        _mixf_body,
        out_shape=(jax.ShapeDtypeStruct((b, s, N_PART), F32),
                   jax.ShapeDtypeStruct((b, s, N_CONV), BF16)),
        grid=(b, s // tb),
        in_specs=[
            pl.BlockSpec((None, tb, N_ACT), row),
            pl.BlockSpec((None, HALO, N_CONV), lambda bi, i: (bi, jnp.maximum(i * per - 1, 0), 0)),
            pl.BlockSpec((None, HALO, N_CONV), lambda bi, i: (bi, jnp.minimum((i + 1) * per, last), 0)),
            pl.BlockSpec((None, tb, LANES), row),
            pl.BlockSpec((CONV_W, N_CONV), const),
            pl.BlockSpec((1, N_CONV), const),
            pl.BlockSpec((D_GROUP, 2 * D_GROUP), const),
            pl.BlockSpec((1, 2 * D_GROUP), const),
            pl.BlockSpec((1, D_GROUP), const),
            pl.BlockSpec((1, LANES), const),
            pl.BlockSpec((1, LANES), const),
            pl.BlockSpec((1, D_GROUP), const),
        ],
        out_specs=(pl.BlockSpec((None, tb, N_PART), row),
                   pl.BlockSpec((None, tb, N_CONV), row)),
        scratch_shapes=[
            pltpu.VMEM((tb, SSD_XBC), F32),
            pltpu.VMEM((tb, LANES), F32),
            pltpu.VMEM((1, D_GROUP), F32),
            pltpu.VMEM((SSD_GROUPS, SSD_STATE, SSD_HPG * SSD_HEAD_DIM), F32),
            pltpu.VMEM((RET_HEADS, RET_HEAD_DIM, RET_HEAD_DIM), F32),
        ],
        compiler_params=pltpu.CompilerParams(
            dimension_semantics=("arbitrary", "arbitrary"), vmem_limit_bytes=VMEM_LIMIT),
        name="mix_fwd",
    )(act, act, act, dtr, cw, cb, wg, bg, lam, dtb, alog, dsk)


def _mixb_body(x_ref, act_ref, cv_ref, gate_ref, dtr_ref, part_ref,
               wg_ref, bg_ref, lam_ref, dtb_ref, alog_ref,
               snw_ref, rnw_ref, wo_ref, o_ref,
               dts_ref, y_ref, hc_ref, sb_ref, rb_ref):
    i = pl.program_id(1)
    tb = act_ref.shape[0]

    @pl.when(i == 0)
    def _():
        hc_ref[...] = jnp.zeros(hc_ref.shape, F32)
        sb_ref[...] = jnp.zeros(sb_ref.shape, F32)
        rb_ref[...] = jnp.zeros(rb_ref.shape, F32)

    xc = cv_ref[:, 0:D_GROUP].astype(F32)
    a, u = _lru_inputs(xc, wg_ref, bg_ref, lam_ref)
    h, hc_ref[...] = _lru_scan(a, u, hc_ref[...], reverse=True)
    gate = gate_ref[:, 0:D_GROUP].astype(F32)
    y_ref[:, 0:D_GROUP] = (part_ref[:, 0:D_GROUP] + h) * _gelu_tanh(gate)

    dts_ref[...] = _softplus(dtr_ref[...] + dtb_ref[...])

    tr, tc = _time_consts()
    ltri = jnp.where(tr >= tc, 1.0, 0.0)
    pos = tr[:, 0:1].astype(F32)
    expand = _expand_mat(SSD_HEADS)
    expand_b = expand.astype(BF16)
    a_row = _neg_exp_alog(alog_ref)
    last = CHUNK - 1
    ret_q = [jnp.exp(_ret_gamma_log(hd) * (CHUNK - pos)) for hd in range(RET_HEADS)]
    ret_k = [jnp.exp(_ret_gamma_log(hd) * pos) for hd in range(RET_HEADS)]
    nck = tb // CHUNK

    def chunk(j, carry):
        c = nck - 1 - j
        rs = pl.ds(pl.multiple_of(c * CHUNK, CHUNK), CHUNK)
        xb = cv_ref[rs, D_GROUP:2 * D_GROUP]
        bm = cv_ref[rs, 2 * D_GROUP:2 * D_GROUP + LANES]
        cm = cv_ref[rs, 2 * D_GROUP + LANES:N_CONV]
        dt = dts_ref[rs, :]
        da = dt * a_row
        cum = _dot_hi(ltri, da)
        tot = cum[last:last + 1, :]
        rcum = tot - cum + da
        e2 = _dot(jnp.concatenate([jnp.exp(tot - rcum) * dt, jnp.exp(rcum)], axis=0).astype(BF16), expand_b)
        xd = (xb.astype(F32) * e2[0:CHUNK, :]).astype(BF16)
        ecum_x = e2[CHUNK:2 * CHUNK, :]
        cdec_x = _dot_hi(jnp.broadcast_to(jnp.exp(tot), (SUBLANES, LANES)), expand)[0:1, :]
        yo = []
        for g in range(SSD_GROUPS):
            gs = slice(g * SSD_STATE, (g + 1) * SSD_STATE)
            hs = slice(g * 256, (g + 1) * 256)
            yo.append(_dot(cm[:, gs], sb_ref[g].astype(BF16)) * ecum_x[:, hs])
            st = lax.dot_general(bm[:, gs], xd[:, hs], _TN, preferred_element_type=F32)
            sb_ref[g] = sb_ref[g] * cdec_x[:, hs] + st
        y_ref[rs, D_GROUP:2 * D_GROUP] = part_ref[rs, D_GROUP:2 * D_GROUP] + jnp.concatenate(yo, axis=1)

        yr = []
        for hd in range(RET_HEADS):
            q = act_ref[rs, Q_OFF + hd * RET_HEAD_DIM:Q_OFF + (hd + 1) * RET_HEAD_DIM]
            k = act_ref[rs, K_OFF + hd * RET_HEAD_DIM:K_OFF + (hd + 1) * RET_HEAD_DIM]
            v = act_ref[rs, V_OFF + hd * RET_HEAD_DIM:V_OFF + (hd + 1) * RET_HEAD_DIM]
            yr.append(_dot(q, rb_ref[hd].astype(BF16)) * ret_q[hd])
            kd = (k.astype(F32) * ret_k[hd]).astype(BF16)
            kv = lax.dot_general(kd, v, _TN, preferred_element_type=F32)
            rb_ref[hd] = rb_ref[hd] * math.exp(_ret_gamma_log(hd) * CHUNK) + kv
        y_ref[rs, 2 * D_GROUP:3 * D_GROUP] = part_ref[rs, 2 * D_GROUP:3 * D_GROUP] + jnp.concatenate(yr, axis=1)
        return carry

    lax.fori_loop(0, nck, chunk, 0)

    ys = y_ref[:, D_GROUP:2 * D_GROUP] * _silu(gate_ref[:, D_GROUP:2 * D_GROUP].astype(F32))
    ys = ys * lax.rsqrt(jnp.mean(ys * ys, axis=-1, keepdims=True) + EPS) * snw_ref[...]
    y_ref[:, D_GROUP:2 * D_GROUP] = ys
    for hd in range(RET_HEADS):
        cs = slice(2 * D_GROUP + hd * RET_HEAD_DIM, 2 * D_GROUP + (hd + 1) * RET_HEAD_DIM)
        ws = slice(hd * RET_HEAD_DIM, (hd + 1) * RET_HEAD_DIM)
        y = y_ref[:, cs]
        mu = jnp.mean(y, axis=-1, keepdims=True)
        yc = y - mu
        var = jnp.mean(yc * yc, axis=-1, keepdims=True)
        y_ref[:, cs] = yc * lax.rsqrt(var + EPS) * rnw_ref[:, ws] * _silu(gate_ref[:, cs].astype(F32))
    o_ref[...] = x_ref[...] + _dot(y_ref[...].astype(BF16), wo_ref[...])


def _mix_bwd(x, act, cv, gate, dtr, part, wg, bg, lam, dtb, alog, snw, rnw, wo):
    b, s, _ = act.shape
    tb = min(MIX_TOKENS, s)
    nb = s // tb
    const = lambda bi, i: (0, 0)
    row = lambda bi, i: (bi, nb - 1 - i, 0)
    return pl.pallas_call(
        _mixb_body,
        out_shape=jax.ShapeDtypeStruct((b, s, D_MODEL), F32),
        grid=(b, nb),
        in_specs=[
            pl.BlockSpec((None, tb, D_MODEL), row),
            pl.BlockSpec((None, tb, N_ACT), row),
            pl.BlockSpec((None, tb, N_CONV), row),
            pl.BlockSpec((None, tb, N_GATE), row),
            pl.BlockSpec((None, tb, LANES), row),
            pl.BlockSpec((None, tb, N_PART), row),
            pl.BlockSpec((D_GROUP, 2 * D_GROUP), const),
            pl.BlockSpec((1, 2 * D_GROUP), const),
            pl.BlockSpec((1, D_GROUP), const),
            pl.BlockSpec((1, LANES), const),
            pl.BlockSpec((1, LANES), const),
            pl.BlockSpec((1, D_GROUP), const),
            pl.BlockSpec((1, D_GROUP), const),
            pl.BlockSpec((N_PART, D_MODEL), const),
        ],
        out_specs=pl.BlockSpec((None, tb, D_MODEL), row),
        scratch_shapes=[
            pltpu.VMEM((tb, LANES), F32),
            pltpu.VMEM((tb, N_PART), F32),
            pltpu.VMEM((1, D_GROUP), F32),
            pltpu.VMEM((SSD_GROUPS, SSD_STATE, SSD_HPG * SSD_HEAD_DIM), F32),
            pltpu.VMEM((RET_HEADS, RET_HEAD_DIM, RET_HEAD_DIM), F32),
        ],
        compiler_params=pltpu.CompilerParams(
            dimension_semantics=("arbitrary", "arbitrary"), vmem_limit_bytes=VMEM_LIMIT),
        name="mix_bwd",
    )(x, act, cv, gate, dtr, part, wg, bg, lam, dtb, alog, snw, rnw, wo)


def _norm_body(x_ref, w_ref, o_ref):
    o_ref[...] = _rms(x_ref[...], w_ref[...])


def _final_norm(x, w):
    t = x.shape[0]
    tm = min(1024, t)
    return pl.pallas_call(
        _norm_body,
        out_shape=jax.ShapeDtypeStruct(x.shape, F32),
        grid=(t // tm,),
        in_specs=[pl.BlockSpec((tm, D_MODEL), lambda i: (i, 0)),
                  pl.BlockSpec((1, D_MODEL), lambda i: (0, 0))],
        out_specs=pl.BlockSpec((tm, D_MODEL), lambda i: (i, 0)),
        compiler_params=pltpu.CompilerParams(dimension_semantics=("arbitrary",)),
        name="final_norm",
    )(x, w)


def _to_chunk_order(x):
    b, s, d = x.shape
    return x.reshape(b, s // CHUNK, SUBLANES, SEG, d).swapaxes(2, 3).reshape(b, s, d)


def _from_chunk_order(x):
    b, s, d = x.shape
    return x.reshape(b, s // CHUNK, SEG, SUBLANES, d).swapaxes(2, 3).reshape(b, s, d)


def _block_diag(w):
    nb, bw, _ = w.shape
    eye = jnp.eye(nb, dtype=w.dtype)
    return (eye[:, None, :, None] * w[:, :, None, :]).reshape(nb * bw, nb * bw)


def _pad_lanes(v):
    return jnp.pad(v, (0, LANES - v.shape[0]))[None, :]


def _prep_layer(l, p):
    w_in = p['w_in'][l]
    o = [0, 512, 1024, 1536, 2304, 2320, 2832, 3344, 3856, 4368]
    lru_x, lru_gate, ssd_z, xbc, dtc, q, k, v, g = [w_in[:, o[j]:o[j + 1]] for j in range(9)]
    gate_w = []
    gate_b = []
    for d in range(2):
        gate_w.append(jnp.concatenate([_block_diag(p['lru_w_a'][l, d]), _block_diag(p['lru_w_i'][l, d])], axis=1))
        gate_b.append(jnp.concatenate([p['lru_b_a'][l, d], p['lru_b_i'][l, d]])[None, :])
    return dict(
        ffn1_nw=p['ffn1_norm'][l][None, :], ffn1_wgu=p['ffn1_w_gu'][l].astype(BF16), ffn1_wd=p['ffn1_w_down'][l].astype(BF16),
        ffn2_nw=p['ffn2_norm'][l][None, :], ffn2_wgu=p['ffn2_w_gu'][l].astype(BF16), ffn2_wd=p['ffn2_w_down'][l].astype(BF16),
        mix_nw=p['mix_norm'][l][None, :],
        wa=jnp.concatenate([lru_x, xbc, q, k, v], axis=1).astype(BF16),
        wg=jnp.concatenate([lru_gate, ssd_z, g], axis=1).astype(BF16),
        wdt=jnp.pad(dtc, ((0, 0), (0, LANES - dtc.shape[1]))).astype(BF16),
        cw=jnp.concatenate([p['lru_conv_w'][l], p['ssd_conv_w'][l]], axis=1),
        cb=jnp.concatenate([p['lru_conv_b'][l], p['ssd_conv_b'][l]])[None, :],
        gate_w=[w.astype(BF16) for w in gate_w], gate_b=gate_b,
        lam=[p['lru_lam'][l, d][None, :] for d in range(2)],
        dtb=_pad_lanes(p['ssd_dt_bias'][l].reshape(-1)),
        alog=_pad_lanes(p['ssd_a_log'][l].reshape(-1)),
        dsk=jnp.repeat(p['ssd_d'][l], SSD_HEAD_DIM)[None, :],
        snw=p['ssd_norm'][l][None, :], rnw=p['ret_norm'][l][None, :],
        wo=p['w_out'][l].astype(BF16),
    )


def _rope_tables(s):
    d = RET_HEAD_DIM
    inv_freq = 1.0 / (ROPE_BASE ** (jnp.arange(0, d, 2, dtype=F32) / d))
    r = jnp.arange(s, dtype=jnp.int32)
    local = r % CHUNK
    pos = (r // CHUNK) * CHUNK + (local % SUBLANES) * SEG + local // SUBLANES
    ang = pos.astype(F32)[:, None] * inv_freq[None, :]
    cos = jnp.cos(ang)
    sin = jnp.sin(ang)
    return jnp.concatenate([cos, cos], axis=1), jnp.concatenate([-sin, sin], axis=1)


def _trunk(x, layers, final_nw):
    b, s, d = x.shape
    c2, s2 = _rope_tables(s)
    x = _to_chunk_order(x)
    for lp in layers:
        x = _ffn(x.reshape(b * s, d), lp['ffn1_nw'], lp['ffn1_wgu'], lp['ffn1_wd']).reshape(b, s, d)
        act, gate, dtr = _proj(x, lp['mix_nw'], lp['wa'], lp['wg'], lp['wdt'], c2, s2)
        part, cv = _mix_fwd(act, dtr, lp['cw'], lp['cb'], lp['gate_w'][0], lp['gate_b'][0], lp['lam'][0],
                            lp['dtb'], lp['alog'], lp['dsk'])
        x = _mix_bwd(x, act, cv, gate, dtr, part, lp['gate_w'][1], lp['gate_b'][1], lp['lam'][1],
                     lp['dtb'], lp['alog'], lp['snw'], lp['rnw'], lp['wo'])
        x = _ffn(x.reshape(b * s, d), lp['ffn2_nw'], lp['ffn2_wgu'], lp['ffn2_wd']).reshape(b, s, d)
    return _from_chunk_order(_final_norm(x.reshape(b * s, d), final_nw).reshape(b, s, d))


def kernel(x_prompt, x_sample, ffn1_norm, ffn1_w_gu, ffn1_w_down, mix_norm, w_in, lru_conv_w, lru_conv_b, lru_w_a, lru_b_a, lru_w_i, lru_b_i, lru_lam, ssd_conv_w, ssd_conv_b, ssd_dt_bias, ssd_a_log, ssd_d, ssd_norm, ret_norm, w_out, ffn2_norm, ffn2_w_gu, ffn2_w_down, final_norm):
    p = dict(ffn1_norm=ffn1_norm, ffn1_w_gu=ffn1_w_gu, ffn1_w_down=ffn1_w_down, mix_norm=mix_norm, w_in=w_in,
             lru_conv_w=lru_conv_w, lru_conv_b=lru_conv_b, lru_w_a=lru_w_a, lru_b_a=lru_b_a, lru_w_i=lru_w_i,
             lru_b_i=lru_b_i, lru_lam=lru_lam, ssd_conv_w=ssd_conv_w, ssd_conv_b=ssd_conv_b, ssd_dt_bias=ssd_dt_bias,
             ssd_a_log=ssd_a_log, ssd_d=ssd_d, ssd_norm=ssd_norm, ret_norm=ret_norm, w_out=w_out,
             ffn2_norm=ffn2_norm, ffn2_w_gu=ffn2_w_gu, ffn2_w_down=ffn2_w_down)
    layers = [_prep_layer(l, p) for l in range(ffn1_norm.shape[0])]
    final_nw = final_norm[None, :]
    return (_trunk(x_prompt, layers, final_nw), _trunk(x_sample, layers, final_nw))
```

```python
import math

import jax
import jax.numpy as jnp
from jax import lax
from jax.experimental import pallas as pl
from jax.experimental.pallas import tpu as pltpu

F32 = jnp.float32
BF16 = jnp.bfloat16

D_MODEL = 1024
D_FF = 2816
D_GROUP = 512
EPS = 1e-6
CONV_W = 4
LRU_C = 8.0
SSD_HEADS = 8
SSD_HEAD_DIM = 64
SSD_STATE = 64
SSD_GROUPS = 2
SSD_HPG = SSD_HEADS // SSD_GROUPS
SSD_XBC = D_GROUP + 2 * SSD_GROUPS * SSD_STATE
RET_HEADS = 4
RET_HEAD_DIM = 128
CHUNK = 128
ROPE_BASE = 10000.0

LANES = 128
SUBLANES = 8
BF16_SUBLANES = 16
SEG = CHUNK // SUBLANES
HALO = BF16_SUBLANES
VMEM_LIMIT = 56 * 1024 * 1024

N_CONV = D_GROUP + SSD_XBC
N_ACT = N_CONV + 3 * D_GROUP
N_GATE = 3 * D_GROUP
Q_OFF = N_CONV
K_OFF = N_CONV + D_GROUP
V_OFF = N_CONV + 2 * D_GROUP
N_PART = 3 * D_GROUP

FFN_TOKENS = 1024
PROJ_TOKENS = 1024
MIX_TOKENS = 512
FF_CHUNK = 512
TINY = 1e-30

_NT = (((1,), (1,)), ((), ()))
_TN = (((0,), (0,)), ((), ()))


def _dot(a, b):
    return jnp.dot(a, b, preferred_element_type=F32)


def _dot_hi(a, b):
    return jnp.dot(a, b, preferred_element_type=F32, precision=lax.Precision.HIGHEST)


def _rms(x, w):
    return x * lax.rsqrt(jnp.mean(x * x, axis=-1, keepdims=True) + EPS) * w


def _sigmoid(x):
    return 0.5 * jnp.tanh(0.5 * x) + 0.5


def _silu(x):
    h = 0.5 * x
    return h * jnp.tanh(h) + h


def _cumsum_rows(tri3, x):
    hi = x.astype(BF16)
    r1 = x - hi.astype(F32)
    mid = r1.astype(BF16)
    lo = (r1 - mid.astype(F32)).astype(BF16)
    return _dot(tri3, jnp.concatenate([hi, mid, lo], axis=0))


def _softplus(x):
    return jnp.maximum(x, 0.0) + jnp.log(1.0 + jnp.exp(-jnp.abs(x)))


def _gelu_tanh(x):
    c = math.sqrt(2.0 / math.pi)
    return 0.5 * x * (1.0 + jnp.tanh(c * (x + 0.044715 * (x * x * x))))


def _ffn_body(x_ref, nw_ref, wgu_ref, wd_ref, o_ref):
    x = x_ref[...].reshape(-1, D_MODEL)
    xb = _rms(x, nw_ref[...]).astype(BF16)
    acc = jnp.zeros(x.shape, F32)
    for c in range(0, D_FF, FF_CHUNK):
        w = min(FF_CHUNK, D_FF - c)
        g = _dot(xb, wgu_ref[:, c:c + w])
        u = _dot(xb, wgu_ref[:, D_FF + c:D_FF + c + w])
        a = (_silu(g) * u).astype(BF16)
        acc = acc + _dot(a, wd_ref[c:c + w, :])
    o_ref[...] = (x + 0.5 * acc).reshape(o_ref.shape)


def _ffn_weight_specs():
    const = lambda *_: (0, 0)
    return [
        pl.BlockSpec((1, D_MODEL), const),
        pl.BlockSpec((D_MODEL, 2 * D_FF), const, pipeline_mode=pl.Buffered(1)),
        pl.BlockSpec((D_FF, D_MODEL), const, pipeline_mode=pl.Buffered(1)),
    ]


def _ffn(x, nw, wgu, wd):
    t = x.shape[0]
    tm = min(FFN_TOKENS, t)
    return pl.pallas_call(
        _ffn_body,
        out_shape=jax.ShapeDtypeStruct(x.shape, F32),
        grid=(t // tm,),
        in_specs=[pl.BlockSpec((tm, D_MODEL), lambda i: (i, 0))] + _ffn_weight_specs(),
        out_specs=pl.BlockSpec((tm, D_MODEL), lambda i: (i, 0)),
        compiler_params=pltpu.CompilerParams(
            dimension_semantics=("arbitrary",), vmem_limit_bytes=VMEM_LIMIT),
        name="ffn",
    )(x, nw, wgu, wd)


def _proj_body(x_ref, nw_ref, wa_ref, wg_ref, wdt_ref, c2_ref, s2_ref, a_ref, g_ref, dt_ref):
    xb = _rms(x_ref[...], nw_ref[...]).astype(BF16)
    pa = _dot(xb, wa_ref[...])
    a_ref[:, 0:Q_OFF] = pa[:, 0:Q_OFF].astype(BF16)
    a_ref[:, V_OFF:N_ACT] = pa[:, V_OFF:N_ACT].astype(BF16)
    c2 = c2_ref[...]
    s2 = s2_ref[...]
    kscale = RET_HEAD_DIM ** -0.5
    for h in range(2 * RET_HEADS):
        lo = Q_OFF + h * RET_HEAD_DIM
        blk = pa[:, lo:lo + RET_HEAD_DIM]
        rot = blk * c2 + pltpu.roll(blk, RET_HEAD_DIM // 2, 1) * s2
        if h >= RET_HEADS:
            rot = rot * kscale
        a_ref[:, lo:lo + RET_HEAD_DIM] = rot.astype(BF16)
    g_ref[...] = _dot(xb, wg_ref[...]).astype(BF16)
    dt_ref[...] = _dot(xb, wdt_ref[...])


def _proj(x, nw, wa, wg, wdt, c2, s2):
    b, s, _ = x.shape
    tm = min(PROJ_TOKENS, s)
    const = lambda bi, i: (0, 0)
    row = lambda bi, i: (bi, i, 0)
    return pl.pallas_call(
        _proj_body,
        out_shape=(jax.ShapeDtypeStruct((b, s, N_ACT), BF16),
                   jax.ShapeDtypeStruct((b, s, N_GATE), BF16),
                   jax.ShapeDtypeStruct((b, s, LANES), F32)),
        grid=(b, s // tm),
        in_specs=[
            pl.BlockSpec((None, tm, D_MODEL), row),
            pl.BlockSpec((1, D_MODEL), const),
            pl.BlockSpec((D_MODEL, N_ACT), const, pipeline_mode=pl.Buffered(1)),
            pl.BlockSpec((D_MODEL, N_GATE), const, pipeline_mode=pl.Buffered(1)),
            pl.BlockSpec((D_MODEL, LANES), const, pipeline_mode=pl.Buffered(1)),
            pl.BlockSpec((tm, LANES), lambda bi, i: (i, 0)),
            pl.BlockSpec((tm, LANES), lambda bi, i: (i, 0)),
        ],
        out_specs=(pl.BlockSpec((None, tm, N_ACT), row),
                   pl.BlockSpec((None, tm, N_GATE), row),
                   pl.BlockSpec((None, tm, LANES), row)),
        compiler_params=pltpu.CompilerParams(
            dimension_semantics=("arbitrary", "arbitrary"), vmem_limit_bytes=VMEM_LIMIT),
        name="mix_proj",
    )(x, nw, wa, wg, wdt, c2, s2)


def _split(x):
    return x.reshape(x.shape[0] // CHUNK, SEG, SUBLANES, x.shape[1])


def _seg_rows(x4, j):
    return x4[:, j].reshape(-1, x4.shape[3])


def _conv(x, prev, nxt, cw_ref, cb_ref):
    x4 = _split(x)
    nck, _, _, ch = x4.shape
    nseg = nck * SUBLANES
    rows = lax.broadcasted_iota(jnp.int32, (nseg, ch), 0)
    n1 = jnp.where(rows == nseg - 1, nxt[0:1, :], pltpu.roll(_seg_rows(x4, 0), nseg - 1, 0))
    p1 = jnp.where(rows == 0, prev[HALO - 1:HALO, :], pltpu.roll(_seg_rows(x4, SEG - 1), 1, 0))
    p2 = jnp.where(rows == 0, prev[HALO - 1 - SUBLANES:HALO - SUBLANES, :],
                   pltpu.roll(_seg_rows(x4, SEG - 2), 1, 0))
    as4 = lambda v: v.reshape(nck, 1, SUBLANES, ch)
    xp1 = jnp.concatenate([x4[:, 1:SEG], as4(n1)], axis=1)
    xm1 = jnp.concatenate([as4(p1), x4[:, 0:SEG - 1]], axis=1)
    xm2 = jnp.concatenate([as4(p2), as4(p1), x4[:, 0:SEG - 2]], axis=1)
    w = cw_ref[...]
    out = cb_ref[...] + xm2 * w[0:1, :] + xm1 * w[1:2, :] + x4 * w[2:3, :] + xp1 * w[3:4, :]
    return out.reshape(x.shape)


def _lru_inputs(xc, wg_ref, bg_ref, lam_ref):
    xb = xc.astype(BF16)
    rs = []
    gs = []
    for p in range(D_GROUP // LANES):
        cs = slice(p * LANES, (p + 1) * LANES)
        g = _dot(xb[:, cs], wg_ref[p])
        rs.append(g[:, 0:LANES])
        gs.append(g[:, LANES:2 * LANES])
    r = _sigmoid(jnp.concatenate(rs, axis=1) + bg_ref[:, 0:D_GROUP])
    ig = _sigmoid(jnp.concatenate(gs, axis=1) + bg_ref[:, D_GROUP:2 * D_GROUP])
    log_a = (-LRU_C) * r * _softplus(-lam_ref[...])
    a = jnp.exp(log_a)
    om = 1.0 - a * a
    u = om * lax.rsqrt(jnp.maximum(om, TINY)) * (ig * xc)
    return a, u


def _row_scan(a, u, reverse):
    n = a.shape[0]
    rows = lax.broadcasted_iota(jnp.int32, a.shape, 0)
    s = 1
    while s < n:
        if reverse:
            a_s = pltpu.roll(a, n - s, 0)
            u_s = pltpu.roll(u, n - s, 0)
            m = rows < n - s
        else:
            a_s = pltpu.roll(a, s, 0)
            u_s = pltpu.roll(u, s, 0)
            m = rows >= s
        u = jnp.where(m, a * u_s + u, u)
        a = jnp.where(m, a * a_s, a)
        s *= 2
    return a, u


def _lru_scan(a, u, carry, reverse):
    a4 = _split(a)
    u4 = _split(u)
    nck, _, _, ch = a4.shape
    nseg = nck * SUBLANES
    order = range(SEG - 1, -1, -1) if reverse else range(SEG)
    acc_a = {}
    acc_u = {}
    pa = pu = None
    for j in order:
        aj = a4[:, j]
        uj = u4[:, j]
        if pa is not None:
            uj = aj * pu + uj
            aj = aj * pa
        acc_a[j], acc_u[j] = aj, uj
        pa, pu = aj, uj
    tot_a, hend = _row_scan(pa.reshape(nseg, ch), pu.reshape(nseg, ch), reverse)
    hend = hend + tot_a * carry
    rows = lax.broadcasted_iota(jnp.int32, (nseg, ch), 0)
    if reverse:
        cin = jnp.where(rows == nseg - 1, carry, pltpu.roll(hend, nseg - 1, 0))
        new_carry = hend[0:1, :]
    else:
        cin = jnp.where(rows == 0, carry, pltpu.roll(hend, 1, 0))
        new_carry = hend[nseg - 1:nseg, :]
    cin = cin.reshape(nck, SUBLANES, ch)
    h = jnp.stack([acc_u[j] + acc_a[j] * cin for j in range(SEG)], axis=1)
    return h.reshape(a.shape), new_carry


def _time_consts():
    r = lax.broadcasted_iota(jnp.int32, (CHUNK, CHUNK), 0)
    c = lax.broadcasted_iota(jnp.int32, (CHUNK, CHUNK), 1)
    tr = (r % SUBLANES) * SEG + r // SUBLANES
    tc = (c % SUBLANES) * SEG + c // SUBLANES
    return tr, tc


def _expand_mat(first_row):
    r = lax.broadcasted_iota(jnp.int32, (LANES, D_GROUP), 0)
    c = lax.broadcasted_iota(jnp.int32, (LANES, D_GROUP), 1)
    return jnp.where(r == first_row + c // SSD_HEAD_DIM, 1.0, 0.0)


def _ret_gamma_log(h):
    return math.log1p(-(2.0 ** (-5.0 - h)))


def _neg_exp_alog(alog_ref):
    lane = lax.broadcasted_iota(jnp.int32, (1, LANES), 1)
    return jnp.where(lane < 2 * SSD_HEADS, -jnp.exp(alog_ref[...]), 0.0)


def _ret_consts(rc_ref, tr, tc):
    dist = jnp.abs(tr - tc).astype(F32)
    pos = tr.astype(F32)
    for hd in range(RET_HEADS):
        lg = _ret_gamma_log(hd)
        rc_ref[hd] = jnp.exp(lg * dist)
        rc_ref[RET_HEADS + hd] = jnp.exp(lg * (pos + 1.0))
        rc_ref[2 * RET_HEADS + hd] = jnp.exp(lg * (CHUNK - 1.0 - pos))
        rc_ref[3 * RET_HEADS + hd] = jnp.exp(lg * (CHUNK - pos))
        rc_ref[4 * RET_HEADS + hd] = jnp.exp(lg * pos)


def _mixf_body(act_ref, prev_ref, next_ref, dtr_ref, cw_ref, cb_ref, wg_ref, bg_ref, lam_ref,
               dtb_ref, alog_ref, dsk_ref, part_ref, cv_ref,
               xbc_ref, dts_ref, hc_ref, sf_ref, rf_ref, rc_ref):
    i = pl.program_id(1)
    n = pl.num_programs(1)
    tb = act_ref.shape[0]
    tr, tc = _time_consts()

    @pl.when(i == 0)
    def _():
        hc_ref[...] = jnp.zeros(hc_ref.shape, F32)
        sf_ref[...] = jnp.zeros(sf_ref.shape, F32)
        rf_ref[...] = jnp.zeros(rf_ref.shape, F32)
        _ret_consts(rc_ref, tr, tc)

    prev = prev_ref[...].astype(F32) * jnp.where(i == 0, 0.0, 1.0)
    nxt = next_ref[...].astype(F32) * jnp.where(i == n - 1, 0.0, 1.0)
    conv = _conv(act_ref[:, 0:N_CONV].astype(F32), prev, nxt, cw_ref, cb_ref)
    xc = conv[:, 0:D_GROUP]
    xbc = _silu(conv[:, D_GROUP:N_CONV])
    xbc_ref[...] = xbc
    cv_ref[:, 0:D_GROUP] = xc.astype(BF16)
    cv_ref[:, D_GROUP:N_CONV] = xbc.astype(BF16)

    a, u = _lru_inputs(xc, wg_ref, bg_ref, lam_ref)
    h, hc_ref[...] = _lru_scan(a, u, hc_ref[...], reverse=False)
    part_ref[:, 0:D_GROUP] = h

    dts_ref[...] = _softplus(dtr_ref[...] + dtb_ref[...])

    lower = tr > tc
    diag = tr == tc
    ltri = jnp.where(tr >= tc, 1.0, 0.0)
    ltri3 = jnp.concatenate([ltri, ltri, ltri], axis=1).astype(BF16)
    utri = jnp.where(tr <= tc, 1.0, 0.0)
    expand = _expand_mat(0)
    a_row = _neg_exp_alog(alog_ref)
    last = CHUNK - 1
    lane = lax.broadcasted_iota(jnp.int32, (CHUNK, LANES), 1)
    first_head = lane < SSD_HEAD_DIM
    nck = tb // CHUNK
    rows = [slice(c * CHUNK, (c + 1) * CHUNK) for c in range(nck)]
    lanes = rows

    dt_all = jnp.concatenate([dts_ref[r, :] for r in rows], axis=1)
    da_all = dt_all * jnp.concatenate([a_row] * nck, axis=1)
    cum_all = _cumsum_rows(ltri3, da_all)
    tot_all = cum_all[last:last + 1, :]
    rcum_all = tot_all - cum_all + da_all
    da_t = jnp.concatenate([da_all[:, l].T[0:BF16_SUBLANES, :] for l in lanes], axis=0)
    dt_t = jnp.concatenate([dt_all[:, l].T[0:BF16_SUBLANES, :] for l in lanes], axis=0)
    cum_t = _dot_hi(da_t, utri)
    rcum_t = cum_t[:, last:last + 1] - cum_t + da_t
    ldt = jnp.log(dt_t)
    sub_f = cum_t - ldt
    sub_b = rcum_t - ldt
    e_rows = []
    for l in lanes:
        e_rows += [jnp.exp(tot_all[:, l] - cum_all[:, l]) * dt_all[:, l], jnp.exp(cum_all[:, l])]
    e2 = _dot(jnp.concatenate(e_rows, axis=0).astype(BF16), expand.astype(BF16))
    cdec = _dot_hi(jnp.concatenate([jnp.broadcast_to(jnp.exp(tot_all[:, l]), (SUBLANES, LANES)) for l in lanes],
                                   axis=0), expand)

    y_ssd = []
    st_all = []
    for c in range(nck):
        x = xbc_ref[rows[c], 0:D_GROUP]
        xb = x.astype(BF16)
        bm = xbc_ref[rows[c], D_GROUP:D_GROUP + LANES].astype(BF16)
        cm = xbc_ref[rows[c], D_GROUP + LANES:SSD_XBC].astype(BF16)
        xd = (x * e2[2 * c * CHUNK:(2 * c + 1) * CHUNK, :]).astype(BF16)
        cum = cum_all[:, lanes[c]]
        rcum = rcum_all[:, lanes[c]]
        ldiag = jnp.log(dt_t[16 * c:16 * c + SSD_HEADS, :] + dt_t[16 * c + SSD_HEADS:16 * (c + 1), :])
        ys = []
        sts = []
        for g in range(SSD_GROUPS):
            gs = slice(g * SSD_STATE, (g + 1) * SSD_STATE)
            cb = lax.dot_general(cm[:, gs], bm[:, gs], _NT, preferred_element_type=F32)
            for pp in range(SSD_HPG // 2):
                ws = []
                for hd in (g * SSD_HPG + 2 * pp, g * SSD_HPG + 2 * pp + 1):
                    hb = SSD_HEADS + hd
                    rf = 16 * c + hd
                    rb = 16 * c + hb
                    z = jnp.where(lower, cum[:, hd:hd + 1] - sub_f[rf:rf + 1, :],
                                  jnp.where(diag, ldiag[hd:hd + 1, :], rcum[:, hb:hb + 1] - sub_b[rb:rb + 1, :]))
                    ws.append((cb * jnp.exp(z)).astype(BF16))
                p0 = (g * SSD_HPG + 2 * pp) * SSD_HEAD_DIM
                xp = xb[:, p0:p0 + LANES]
                zero = jnp.zeros_like(xp)
                xdiag = jnp.concatenate([jnp.where(first_head, xp, zero), jnp.where(first_head, zero, xp)], axis=0)
                ys.append(_dot(jnp.concatenate(ws, axis=1), xdiag))
            sts.append(lax.dot_general(bm[:, gs], xd[:, g * 256:(g + 1) * 256], _TN, preferred_element_type=F32))
        y_ssd.append(jnp.concatenate(ys, axis=1) + dsk_ref[...] * x)
        st_all.append(sts)

    y_ret = []
    kv_all = []
    zeros_h = jnp.zeros((CHUNK, RET_HEAD_DIM), BF16)
    for c in range(nck):
        ys = []
        kvs = []
        for hp in range(RET_HEADS // 2):
            lo = 2 * hp * RET_HEAD_DIM
            q2 = act_ref[rows[c], Q_OFF + lo:Q_OFF + lo + 2 * RET_HEAD_DIM]
            k2 = act_ref[rows[c], K_OFF + lo:K_OFF + lo + 2 * RET_HEAD_DIM]
            v2 = act_ref[rows[c], V_OFF + lo:V_OFF + lo + 2 * RET_HEAD_DIM]
            ka, kb = k2[:, 0:RET_HEAD_DIM], k2[:, RET_HEAD_DIM:]
            va, vb = v2[:, 0:RET_HEAD_DIM], v2[:, RET_HEAD_DIM:]
            kdiag = jnp.concatenate([jnp.concatenate([ka, zeros_h], axis=1),
                                     jnp.concatenate([zeros_h, kb], axis=1)], axis=0)
            vdiag = jnp.concatenate([jnp.concatenate([va, zeros_h], axis=1),
                                     jnp.concatenate([zeros_h, vb], axis=1)], axis=0)
            sc = lax.dot_general(q2, kdiag, _NT, preferred_element_type=F32)
            dec = jnp.concatenate([rc_ref[2 * hp], rc_ref[2 * hp + 1]], axis=1)
            ys.append(_dot((sc * dec).astype(BF16), vdiag))
            for j, (kk, vv) in enumerate(((ka, va), (kb, vb))):
                kd = (kk.astype(F32) * rc_ref[2 * RET_HEADS + 2 * hp + j]).astype(BF16)
                kvs.append(lax.dot_general(kd, vv, _TN, preferred_element_type=F32))
        y_ret.append(ys)
        kv_all.append(kvs)

    for c in range(nck):
        cm = xbc_ref[rows[c], D_GROUP + LANES:SSD_XBC].astype(BF16)
        ecum_x = e2[(2 * c + 1) * CHUNK:(2 * c + 2) * CHUNK, :]
        cdec_x = cdec[SUBLANES * c:SUBLANES * c + 1, :]
        yo = []
        for g in range(SSD_GROUPS):
            gs = slice(g * SSD_STATE, (g + 1) * SSD_STATE)
            hs = slice(g * 256, (g + 1) * 256)
            yo.append(_dot(cm[:, gs], sf_ref[g].astype(BF16)))
            sf_ref[g] = sf_ref[g] * cdec_x[:, hs] + st_all[c][g]
        part_ref[rows[c], D_GROUP:2 * D_GROUP] = y_ssd[c] + jnp.concatenate(yo, axis=1) * ecum_x
        yr = []
        for hp in range(RET_HEADS // 2):
            lo = 2 * hp * RET_HEAD_DIM
            q2 = act_ref[rows[c], Q_OFF + lo:Q_OFF + lo + 2 * RET_HEAD_DIM]
            for j in range(2):
                hd = 2 * hp + j
                qh = q2[:, j * RET_HEAD_DIM:(j + 1) * RET_HEAD_DIM]
                inter = _dot(qh, rf_ref[hd].astype(BF16)) * rc_ref[RET_HEADS + hd]
                yr.append(y_ret[c][hp][:, j * RET_HEAD_DIM:(j + 1) * RET_HEAD_DIM] + inter)
                rf_ref[hd] = rf_ref[hd] * math.exp(_ret_gamma_log(hd) * CHUNK) + kv_all[c][hd]
        part_ref[rows[c], 2 * D_GROUP:3 * D_GROUP] = jnp.concatenate(yr, axis=1)


def _mix_fwd(act, dtr, cw, cb, wg, bg, lam, dtb, alog, dsk):
    b, s, _ = act.shape
    tb = min(MIX_TOKENS, s)
    per = tb // HALO
    last = s // HALO - 1
    const = lambda bi, i: (0, 0)
    row = lambda bi, i: (bi, i, 0)
    return pl.pallas_call(
        _mixf_body,
        out_shape=(jax.ShapeDtypeStruct((b, s, N_PART), F32),
                   jax.ShapeDtypeStruct((b, s, N_CONV), BF16)),
        grid=(b, s // tb),
        in_specs=[
            pl.BlockSpec((None, tb, N_ACT), row),
            pl.BlockSpec((None, HALO, N_CONV), lambda bi, i: (bi, jnp.maximum(i * per - 1, 0), 0)),
            pl.BlockSpec((None, HALO, N_CONV), lambda bi, i: (bi, jnp.minimum((i + 1) * per, last), 0)),
            pl.BlockSpec((None, tb, LANES), row),
            pl.BlockSpec((CONV_W, N_CONV), const),
            pl.BlockSpec((1, N_CONV), const),
            pl.BlockSpec((D_GROUP // LANES, LANES, 2 * LANES), lambda bi, i: (0, 0, 0)),
            pl.BlockSpec((1, 2 * D_GROUP), const),
            pl.BlockSpec((1, D_GROUP), const),
            pl.BlockSpec((1, LANES), const),
            pl.BlockSpec((1, LANES), const),
            pl.BlockSpec((1, D_GROUP), const),
        ],
        out_specs=(pl.BlockSpec((None, tb, N_PART), row),
                   pl.BlockSpec((None, tb, N_CONV), row)),
        scratch_shapes=[
            pltpu.VMEM((tb, SSD_XBC), F32),
            pltpu.VMEM((tb, LANES), F32),
            pltpu.VMEM((1, D_GROUP), F32),
            pltpu.VMEM((SSD_GROUPS, SSD_STATE, SSD_HPG * SSD_HEAD_DIM), F32),
            pltpu.VMEM((RET_HEADS, RET_HEAD_DIM, RET_HEAD_DIM), F32),
            pltpu.VMEM((5 * RET_HEADS, CHUNK, CHUNK), F32),
        ],
        compiler_params=pltpu.CompilerParams(
            dimension_semantics=("arbitrary", "arbitrary"), vmem_limit_bytes=VMEM_LIMIT),
        name="mix_fwd",
    )(act, act, act, dtr, cw, cb, wg, bg, lam, dtb, alog, dsk)


def _mixb_body(x_ref, act_ref, cv_ref, gate_ref, dtr_ref, part_ref,
               wg_ref, bg_ref, lam_ref, dtb_ref, alog_ref,
               snw_ref, rnw_ref, wo_ref, o_ref,
               dts_ref, y_ref, hc_ref, sb_ref, rb_ref, rc_ref):
    i = pl.program_id(1)
    tb = act_ref.shape[0]
    tr, tc = _time_consts()

    @pl.when(i == 0)
    def _():
        hc_ref[...] = jnp.zeros(hc_ref.shape, F32)
        sb_ref[...] = jnp.zeros(sb_ref.shape, F32)
        rb_ref[...] = jnp.zeros(rb_ref.shape, F32)
        _ret_consts(rc_ref, tr, tc)

    xc = cv_ref[:, 0:D_GROUP].astype(F32)
    a, u = _lru_inputs(xc, wg_ref, bg_ref, lam_ref)
    h, hc_ref[...] = _lru_scan(a, u, hc_ref[...], reverse=True)
    gate = gate_ref[:, 0:D_GROUP].astype(F32)
    y_ref[:, 0:D_GROUP] = (part_ref[:, 0:D_GROUP] + h) * _gelu_tanh(gate)

    dts_ref[...] = _softplus(dtr_ref[...] + dtb_ref[...])

    ltri = jnp.where(tr >= tc, 1.0, 0.0)
    ltri3 = jnp.concatenate([ltri, ltri, ltri], axis=1).astype(BF16)
    expand = _expand_mat(SSD_HEADS)
    expand_b = expand.astype(BF16)
    a_row = _neg_exp_alog(alog_ref)
    last = CHUNK - 1
    nck = tb // CHUNK

    def chunk(j, carry):
        c = nck - 1 - j
        rs = pl.ds(pl.multiple_of(c * CHUNK, CHUNK), CHUNK)
        xb = cv_ref[rs, D_GROUP:2 * D_GROUP]
        bm = cv_ref[rs, 2 * D_GROUP:2 * D_GROUP + LANES]
        cm = cv_ref[rs, 2 * D_GROUP + LANES:N_CONV]
        dt = dts_ref[rs, :]
        da = dt * a_row
        cum = _cumsum_rows(ltri3, da)
        tot = cum[last:last + 1, :]
        rcum = tot - cum + da
        e2 = _dot(jnp.concatenate([jnp.exp(tot - rcum) * dt, jnp.exp(rcum)], axis=0).astype(BF16), expand_b)
        xd = (xb.astype(F32) * e2[0:CHUNK, :]).astype(BF16)
        ecum_x = e2[CHUNK:2 * CHUNK, :]
        cdec_x = _dot_hi(jnp.broadcast_to(jnp.exp(tot), (SUBLANES, LANES)), expand)[0:1, :]
        yo = []
        for g in range(SSD_GROUPS):
            gs = slice(g * SSD_STATE, (g + 1) * SSD_STATE)
            hs = slice(g * 256, (g + 1) * 256)
            yo.append(_dot(cm[:, gs], sb_ref[g].astype(BF16)) * ecum_x[:, hs])
            st = lax.dot_general(bm[:, gs], xd[:, hs], _TN, preferred_element_type=F32)
            sb_ref[g] = sb_ref[g] * cdec_x[:, hs] + st
        y_ref[rs, D_GROUP:2 * D_GROUP] = part_ref[rs, D_GROUP:2 * D_GROUP] + jnp.concatenate(yo, axis=1)

        yr = []
        for hd in range(RET_HEADS):
            q = act_ref[rs, Q_OFF + hd * RET_HEAD_DIM:Q_OFF + (hd + 1) * RET_HEAD_DIM]
            k = act_ref[rs, K_OFF + hd * RET_HEAD_DIM:K_OFF + (hd + 1) * RET_HEAD_DIM]
            v = act_ref[rs, V_OFF + hd * RET_HEAD_DIM:V_OFF + (hd + 1) * RET_HEAD_DIM]
            yr.append(_dot(q, rb_ref[hd].astype(BF16)) * rc_ref[3 * RET_HEADS + hd])
            kd = (k.astype(F32) * rc_ref[4 * RET_HEADS + hd]).astype(BF16)
            kv = lax.dot_general(kd, v, _TN, preferred_element_type=F32)
            rb_ref[hd] = rb_ref[hd] * math.exp(_ret_gamma_log(hd) * CHUNK) + kv
        y_ref[rs, 2 * D_GROUP:3 * D_GROUP] = part_ref[rs, 2 * D_GROUP:3 * D_GROUP] + jnp.concatenate(yr, axis=1)
        return carry

    lax.fori_loop(0, nck, chunk, 0, unroll=True)

    ys = y_ref[:, D_GROUP:2 * D_GROUP] * _silu(gate_ref[:, D_GROUP:2 * D_GROUP].astype(F32))
    ys = ys * lax.rsqrt(jnp.mean(ys * ys, axis=-1, keepdims=True) + EPS) * snw_ref[...]
    y_ref[:, D_GROUP:2 * D_GROUP] = ys
    for hd in range(RET_HEADS):
        cs = slice(2 * D_GROUP + hd * RET_HEAD_DIM, 2 * D_GROUP + (hd + 1) * RET_HEAD_DIM)
        ws = slice(hd * RET_HEAD_DIM, (hd + 1) * RET_HEAD_DIM)
        y = y_ref[:, cs]
        mu = jnp.mean(y, axis=-1, keepdims=True)
        yc = y - mu
        var = jnp.mean(yc * yc, axis=-1, keepdims=True)
        y_ref[:, cs] = yc * lax.rsqrt(var + EPS) * rnw_ref[:, ws] * _silu(gate_ref[:, cs].astype(F32))
    o_ref[...] = x_ref[...] + _dot(y_ref[...].astype(BF16), wo_ref[...])


def _mix_bwd(x, act, cv, gate, dtr, part, wg, bg, lam, dtb, alog, snw, rnw, wo):
    b, s, _ = act.shape
    tb = min(MIX_TOKENS, s)
    nb = s // tb
    const = lambda bi, i: (0, 0)
    row = lambda bi, i: (bi, nb - 1 - i, 0)
    return pl.pallas_call(
        _mixb_body,
        out_shape=jax.ShapeDtypeStruct((b, s, D_MODEL), F32),
        grid=(b, nb),
        in_specs=[
            pl.BlockSpec((None, tb, D_MODEL), row),
            pl.BlockSpec((None, tb, N_ACT), row),
            pl.BlockSpec((None, tb, N_CONV), row),
            pl.BlockSpec((None, tb, N_GATE), row),
            pl.BlockSpec((None, tb, LANES), row),
            pl.BlockSpec((None, tb, N_PART), row),
            pl.BlockSpec((D_GROUP // LANES, LANES, 2 * LANES), lambda bi, i: (0, 0, 0)),
            pl.BlockSpec((1, 2 * D_GROUP), const),
            pl.BlockSpec((1, D_GROUP), const),
            pl.BlockSpec((1, LANES), const),
            pl.BlockSpec((1, LANES), const),
            pl.BlockSpec((1, D_GROUP), const),
            pl.BlockSpec((1, D_GROUP), const),
            pl.BlockSpec((N_PART, D_MODEL), const),
        ],
        out_specs=pl.BlockSpec((None, tb, D_MODEL), row),
        scratch_shapes=[
            pltpu.VMEM((tb, LANES), F32),
            pltpu.VMEM((tb, N_PART), F32),
            pltpu.VMEM((1, D_GROUP), F32),
            pltpu.VMEM((SSD_GROUPS, SSD_STATE, SSD_HPG * SSD_HEAD_DIM), F32),
            pltpu.VMEM((RET_HEADS, RET_HEAD_DIM, RET_HEAD_DIM), F32),
            pltpu.VMEM((5 * RET_HEADS, CHUNK, CHUNK), F32),
        ],
        compiler_params=pltpu.CompilerParams(
            dimension_semantics=("arbitrary", "arbitrary"), vmem_limit_bytes=VMEM_LIMIT),
        name="mix_bwd",
    )(x, act, cv, gate, dtr, part, wg, bg, lam, dtb, alog, snw, rnw, wo)


def _norm_body(x_ref, w_ref, o_ref):
    o_ref[...] = _rms(x_ref[...], w_ref[...])


def _final_norm(x, w):
    t = x.shape[0]
    tm = min(1024, t)
    return pl.pallas_call(
        _norm_body,
        out_shape=jax.ShapeDtypeStruct(x.shape, F32),
        grid=(t // tm,),
        in_specs=[pl.BlockSpec((tm, D_MODEL), lambda i: (i, 0)),
                  pl.BlockSpec((1, D_MODEL), lambda i: (0, 0))],
        out_specs=pl.BlockSpec((tm, D_MODEL), lambda i: (i, 0)),
        compiler_params=pltpu.CompilerParams(dimension_semantics=("arbitrary",)),
        name="final_norm",
    )(x, w)


def _to_chunk_order(x):
    b, s, d = x.shape
    return x.reshape(b, s // CHUNK, SUBLANES, SEG, d).swapaxes(2, 3).reshape(b, s, d)


def _from_chunk_order(x):
    b, s, d = x.shape
    return x.reshape(b, s // CHUNK, SEG, SUBLANES, d).swapaxes(2, 3).reshape(b, s, d)


def _block_diag(w):
    nb, bw, _ = w.shape
    eye = jnp.eye(nb, dtype=w.dtype)
    return (eye[:, None, :, None] * w[:, :, None, :]).reshape(nb * bw, nb * bw)


def _pad_lanes(v):
    return jnp.pad(v, (0, LANES - v.shape[0]))[None, :]


def _prep_layer(l, p):
    w_in = p['w_in'][l]
    o = [0, 512, 1024, 1536, 2304, 2320, 2832, 3344, 3856, 4368]
    lru_x, lru_gate, ssd_z, xbc, dtc, q, k, v, g = [w_in[:, o[j]:o[j + 1]] for j in range(9)]
    gate_w = []
    gate_b = []
    for d in range(2):
        wa = _block_diag(p['lru_w_a'][l, d])
        wi = _block_diag(p['lru_w_i'][l, d])
        gate_w.append(jnp.stack([jnp.concatenate([w[j:j + LANES, j:j + LANES] for w in (wa, wi)], axis=1)
                                 for j in range(0, D_GROUP, LANES)]))
        gate_b.append(jnp.concatenate([p['lru_b_a'][l, d], p['lru_b_i'][l, d]])[None, :])
    return dict(
        ffn1_nw=p['ffn1_norm'][l][None, :], ffn1_wgu=p['ffn1_w_gu'][l].astype(BF16), ffn1_wd=p['ffn1_w_down'][l].astype(BF16),
        ffn2_nw=p['ffn2_norm'][l][None, :], ffn2_wgu=p['ffn2_w_gu'][l].astype(BF16), ffn2_wd=p['ffn2_w_down'][l].astype(BF16),
        mix_nw=p['mix_norm'][l][None, :],
        wa=jnp.concatenate([lru_x, xbc, q, k, v], axis=1).astype(BF16),
        wg=jnp.concatenate([lru_gate, ssd_z, g], axis=1).astype(BF16),
        wdt=jnp.pad(dtc, ((0, 0), (0, LANES - dtc.shape[1]))).astype(BF16),
        cw=jnp.concatenate([p['lru_conv_w'][l], p['ssd_conv_w'][l]], axis=1),
        cb=jnp.concatenate([p['lru_conv_b'][l], p['ssd_conv_b'][l]])[None, :],
        gate_w=[w.astype(BF16) for w in gate_w], gate_b=gate_b,
        lam=[p['lru_lam'][l, d][None, :] for d in range(2)],
        dtb=_pad_lanes(p['ssd_dt_bias'][l].reshape(-1)),
        alog=_pad_lanes(p['ssd_a_log'][l].reshape(-1)),
        dsk=jnp.repeat(p['ssd_d'][l], SSD_HEAD_DIM)[None, :],
        snw=p['ssd_norm'][l][None, :], rnw=p['ret_norm'][l][None, :],
        wo=p['w_out'][l].astype(BF16),
    )


def _rope_tables(s):
    d = RET_HEAD_DIM
    inv_freq = 1.0 / (ROPE_BASE ** (jnp.arange(0, d, 2, dtype=F32) / d))
    r = jnp.arange(s, dtype=jnp.int32)
    local = r % CHUNK
    pos = (r // CHUNK) * CHUNK + (local % SUBLANES) * SEG + local // SUBLANES
    ang = pos.astype(F32)[:, None] * inv_freq[None, :]
    cos = jnp.cos(ang)
    sin = jnp.sin(ang)
    return jnp.concatenate([cos, cos], axis=1), jnp.concatenate([-sin, sin], axis=1)


def _trunk(x, layers, final_nw):
    b, s, d = x.shape
    c2, s2 = _rope_tables(s)
    x = _to_chunk_order(x)
    for lp in layers:
        x = _ffn(x.reshape(b * s, d), lp['ffn1_nw'], lp['ffn1_wgu'], lp['ffn1_wd']).reshape(b, s, d)
        act, gate, dtr = _proj(x, lp['mix_nw'], lp['wa'], lp['wg'], lp['wdt'], c2, s2)
        part, cv = _mix_fwd(act, dtr, lp['cw'], lp['cb'], lp['gate_w'][0], lp['gate_b'][0], lp['lam'][0],
                            lp['dtb'], lp['alog'], lp['dsk'])
        x = _mix_bwd(x, act, cv, gate, dtr, part, lp['gate_w'][1], lp['gate_b'][1], lp['lam'][1],
                     lp['dtb'], lp['alog'], lp['snw'], lp['rnw'], lp['wo'])
        x = _ffn(x.reshape(b * s, d), lp['ffn2_nw'], lp['ffn2_wgu'], lp['ffn2_wd']).reshape(b, s, d)
    return _from_chunk_order(_final_norm(x.reshape(b * s, d), final_nw).reshape(b, s, d))


def kernel(x_prompt, x_sample, ffn1_norm, ffn1_w_gu, ffn1_w_down, mix_norm, w_in, lru_conv_w, lru_conv_b, lru_w_a, lru_b_a, lru_w_i, lru_b_i, lru_lam, ssd_conv_w, ssd_conv_b, ssd_dt_bias, ssd_a_log, ssd_d, ssd_norm, ret_norm, w_out, ffn2_norm, ffn2_w_gu, ffn2_w_down, final_norm):
    p = dict(ffn1_norm=ffn1_norm, ffn1_w_gu=ffn1_w_gu, ffn1_w_down=ffn1_w_down, mix_norm=mix_norm, w_in=w_in,
             lru_conv_w=lru_conv_w, lru_conv_b=lru_conv_b, lru_w_a=lru_w_a, lru_b_a=lru_b_a, lru_w_i=lru_w_i,
             lru_b_i=lru_b_i, lru_lam=lru_lam, ssd_conv_w=ssd_conv_w, ssd_conv_b=ssd_conv_b, ssd_dt_bias=ssd_dt_bias,
             ssd_a_log=ssd_a_log, ssd_d=ssd_d, ssd_norm=ssd_norm, ret_norm=ret_norm, w_out=w_out,
             ffn2_norm=ffn2_norm, ffn2_w_gu=ffn2_w_gu, ffn2_w_down=ffn2_w_down)
    layers = [_prep_layer(l, p) for l in range(ffn1_norm.shape[0])]
    final_nw = final_norm[None, :]
    return (_trunk(x_prompt, layers, final_nw), _trunk(x_sample, layers, final_nw))
```

```python
import math

import jax
import jax.numpy as jnp
from jax import lax
from jax.experimental import pallas as pl
from jax.experimental.pallas import tpu as pltpu

F32 = jnp.float32
BF16 = jnp.bfloat16

D_MODEL = 1024
D_FF = 2816
D_GROUP = 512
EPS = 1e-6
CONV_W = 4
LRU_C = 8.0
SSD_HEADS = 8
SSD_HEAD_DIM = 64
SSD_STATE = 64
SSD_GROUPS = 2
SSD_HPG = SSD_HEADS // SSD_GROUPS
SSD_XBC = D_GROUP + 2 * SSD_GROUPS * SSD_STATE
RET_HEADS = 4
RET_HEAD_DIM = 128
CHUNK = 128
ROPE_BASE = 10000.0

LANES = 128
SUBLANES = 8
BF16_SUBLANES = 16
SEG = CHUNK // SUBLANES
HALO = BF16_SUBLANES
VMEM_LIMIT = 56 * 1024 * 1024

N_CONV = D_GROUP + SSD_XBC
N_ACT = N_CONV + 3 * D_GROUP
N_GATE = 3 * D_GROUP
Q_OFF = N_CONV
K_OFF = N_CONV + D_GROUP
V_OFF = N_CONV + 2 * D_GROUP
N_PART = 3 * D_GROUP

FFN_TOKENS = 1024
PROJ_TOKENS = 1024
MIX_TOKENS = 512
FF_CHUNK = 512
TINY = 1e-30

_NT = (((1,), (1,)), ((), ()))
_TN = (((0,), (0,)), ((), ()))


def _dot(a, b):
    return jnp.dot(a, b, preferred_element_type=F32)


def _dot_hi(a, b):
    return jnp.dot(a, b, preferred_element_type=F32, precision=lax.Precision.HIGHEST)


def _rms(x, w):
    return x * lax.rsqrt(jnp.mean(x * x, axis=-1, keepdims=True) + EPS) * w


def _sigmoid(x):
    return 0.5 * jnp.tanh(0.5 * x) + 0.5


def _silu(x):
    h = 0.5 * x
    return h * jnp.tanh(h) + h


def _cumsum_rows(tri3, x):
    hi = x.astype(BF16)
    r1 = x - hi.astype(F32)
    mid = r1.astype(BF16)
    lo = (r1 - mid.astype(F32)).astype(BF16)
    return _dot(tri3, jnp.concatenate([hi, mid, lo], axis=0))


def _softplus(x):
    return jnp.maximum(x, 0.0) + jnp.log(1.0 + jnp.exp(-jnp.abs(x)))


def _gelu_tanh(x):
    c = math.sqrt(2.0 / math.pi)
    return 0.5 * x * (1.0 + jnp.tanh(c * (x + 0.044715 * (x * x * x))))


def _ffn_body(x_ref, nw_ref, wgu_ref, wd_ref, o_ref):
    x = x_ref[...].reshape(-1, D_MODEL)
    xb = _rms(x, nw_ref[...]).astype(BF16)
    acc = jnp.zeros(x.shape, F32)
    for c in range(0, D_FF, FF_CHUNK):
        w = min(FF_CHUNK, D_FF - c)
        g = _dot(xb, wgu_ref[:, c:c + w])
        u = _dot(xb, wgu_ref[:, D_FF + c:D_FF + c + w])
        a = (_silu(g) * u).astype(BF16)
        acc = acc + _dot(a, wd_ref[c:c + w, :])
    o_ref[...] = (x + 0.5 * acc).reshape(o_ref.shape)


def _ffn_weight_specs():
    const = lambda *_: (0, 0)
    return [
        pl.BlockSpec((1, D_MODEL), const),
        pl.BlockSpec((D_MODEL, 2 * D_FF), const, pipeline_mode=pl.Buffered(1)),
        pl.BlockSpec((D_FF, D_MODEL), const, pipeline_mode=pl.Buffered(1)),
    ]


def _ffn(x, nw, wgu, wd):
    t = x.shape[0]
    tm = min(FFN_TOKENS, t)
    return pl.pallas_call(
        _ffn_body,
        out_shape=jax.ShapeDtypeStruct(x.shape, F32),
        grid=(t // tm,),
        in_specs=[pl.BlockSpec((tm, D_MODEL), lambda i: (i, 0))] + _ffn_weight_specs(),
        out_specs=pl.BlockSpec((tm, D_MODEL), lambda i: (i, 0)),
        compiler_params=pltpu.CompilerParams(
            dimension_semantics=("arbitrary",), vmem_limit_bytes=VMEM_LIMIT),
        name="ffn",
    )(x, nw, wgu, wd)


def _proj_body(x_ref, nw_ref, wa_ref, wg_ref, wdt_ref, c2_ref, s2_ref, a_ref, g_ref, dt_ref):
    xb = _rms(x_ref[...], nw_ref[...]).astype(BF16)
    pa = _dot(xb, wa_ref[...])
    a_ref[:, 0:Q_OFF] = pa[:, 0:Q_OFF].astype(BF16)
    a_ref[:, V_OFF:N_ACT] = pa[:, V_OFF:N_ACT].astype(BF16)
    c2 = c2_ref[...]
    s2 = s2_ref[...]
    kscale = RET_HEAD_DIM ** -0.5
    for h in range(2 * RET_HEADS):
        lo = Q_OFF + h * RET_HEAD_DIM
        blk = pa[:, lo:lo + RET_HEAD_DIM]
        rot = blk * c2 + pltpu.roll(blk, RET_HEAD_DIM // 2, 1) * s2
        if h >= RET_HEADS:
            rot = rot * kscale
        a_ref[:, lo:lo + RET_HEAD_DIM] = rot.astype(BF16)
    g_ref[...] = _dot(xb, wg_ref[...]).astype(BF16)
    dt_ref[...] = _dot(xb, wdt_ref[...])


def _proj(x, nw, wa, wg, wdt, c2, s2):
    b, s, _ = x.shape
    tm = min(PROJ_TOKENS, s)
    const = lambda bi, i: (0, 0)
    row = lambda bi, i: (bi, i, 0)
    return pl.pallas_call(
        _proj_body,
        out_shape=(jax.ShapeDtypeStruct((b, s, N_ACT), BF16),
                   jax.ShapeDtypeStruct((b, s, N_GATE), BF16),
                   jax.ShapeDtypeStruct((b, s, LANES), F32)),
        grid=(b, s // tm),
        in_specs=[
            pl.BlockSpec((None, tm, D_MODEL), row),
            pl.BlockSpec((1, D_MODEL), const),
            pl.BlockSpec((D_MODEL, N_ACT), const, pipeline_mode=pl.Buffered(1)),
            pl.BlockSpec((D_MODEL, N_GATE), const, pipeline_mode=pl.Buffered(1)),
            pl.BlockSpec((D_MODEL, LANES), const, pipeline_mode=pl.Buffered(1)),
            pl.BlockSpec((tm, LANES), lambda bi, i: (i, 0)),
            pl.BlockSpec((tm, LANES), lambda bi, i: (i, 0)),
        ],
        out_specs=(pl.BlockSpec((None, tm, N_ACT), row),
                   pl.BlockSpec((None, tm, N_GATE), row),
                   pl.BlockSpec((None, tm, LANES), row)),
        compiler_params=pltpu.CompilerParams(
            dimension_semantics=("arbitrary", "arbitrary"), vmem_limit_bytes=VMEM_LIMIT),
        name="mix_proj",
    )(x, nw, wa, wg, wdt, c2, s2)


def _split(x):
    return x.reshape(x.shape[0] // CHUNK, SEG, SUBLANES, x.shape[1])


def _seg_rows(x4, j):
    return x4[:, j].reshape(-1, x4.shape[3])


def _conv_edges(x, prev, nxt):
    x4 = _split(x)
    nck, _, _, ch = x4.shape
    nseg = nck * SUBLANES
    rows = lax.broadcasted_iota(jnp.int32, (nseg, ch), 0)
    n1 = jnp.where(rows == nseg - 1, nxt[0:1, :], pltpu.roll(_seg_rows(x4, 0), nseg - 1, 0))
    p1 = jnp.where(rows == 0, prev[HALO - 1:HALO, :], pltpu.roll(_seg_rows(x4, SEG - 1), 1, 0))
    p2 = jnp.where(rows == 0, prev[HALO - 1 - SUBLANES:HALO - SUBLANES, :],
                   pltpu.roll(_seg_rows(x4, SEG - 2), 1, 0))
    return n1, p1, p2


def _conv_chunk(x, n1, p1, p2, cw_ref, cb_ref):
    x3 = x.reshape(SEG, SUBLANES, x.shape[1])
    xp1 = jnp.concatenate([x3[1:SEG], n1[None]], axis=0)
    xm1 = jnp.concatenate([p1[None], x3[0:SEG - 1]], axis=0)
    xm2 = jnp.concatenate([p2[None], p1[None], x3[0:SEG - 2]], axis=0)
    w = cw_ref[...]
    out = cb_ref[...] + xm2 * w[0:1, :] + xm1 * w[1:2, :] + x3 * w[2:3, :] + xp1 * w[3:4, :]
    return out.reshape(x.shape)


def _lru_gate_matmul(xc, wg_ref):
    xb = xc.astype(BF16)
    rs = []
    gs = []
    for p in range(D_GROUP // LANES):
        g = _dot(xb[:, p * LANES:(p + 1) * LANES], wg_ref[p])
        rs.append(g[:, 0:LANES])
        gs.append(g[:, LANES:2 * LANES])
    return jnp.concatenate(rs, axis=1), jnp.concatenate(gs, axis=1)


def _lru_coeffs(xc, r_pre, i_pre, bg_ref, sp):
    r = _sigmoid(r_pre + bg_ref[:, 0:D_GROUP])
    ig = _sigmoid(i_pre + bg_ref[:, D_GROUP:2 * D_GROUP])
    a = jnp.exp((-LRU_C) * r * sp)
    om = 1.0 - a * a
    u = om * lax.rsqrt(jnp.maximum(om, TINY)) * (ig * xc)
    return a, u


def _row_scan(a, u, reverse):
    n = a.shape[0]
    rows = lax.broadcasted_iota(jnp.int32, a.shape, 0)
    s = 1
    while s < n:
        if reverse:
            a_s = pltpu.roll(a, n - s, 0)
            u_s = pltpu.roll(u, n - s, 0)
            m = rows < n - s
        else:
            a_s = pltpu.roll(a, s, 0)
            u_s = pltpu.roll(u, s, 0)
            m = rows >= s
        u = jnp.where(m, a * u_s + u, u)
        a = jnp.where(m, a * a_s, a)
        s *= 2
    return a, u


def _seg_scan(a, u, reverse):
    a3 = a.reshape(SEG, SUBLANES, a.shape[1])
    u3 = u.reshape(SEG, SUBLANES, u.shape[1])
    acc_a = [None] * SEG
    acc_u = [None] * SEG
    pa = pu = None
    for j in (range(SEG - 1, -1, -1) if reverse else range(SEG)):
        aj = a3[j]
        uj = u3[j]
        if pa is not None:
            uj = aj * pu + uj
            aj = aj * pa
        acc_a[j], acc_u[j] = aj, uj
        pa, pu = aj, uj
    return acc_a, acc_u


def _lru_link(scans, carry, reverse):
    end = 0 if reverse else SEG - 1
    tot_a = jnp.concatenate([sc[0][end] for sc in scans], axis=0)
    tot_u = jnp.concatenate([sc[1][end] for sc in scans], axis=0)
    nseg = tot_a.shape[0]
    pa, hend = _row_scan(tot_a, tot_u, reverse)
    hend = hend + pa * carry
    rows = lax.broadcasted_iota(jnp.int32, hend.shape, 0)
    if reverse:
        cin = jnp.where(rows == nseg - 1, carry, pltpu.roll(hend, nseg - 1, 0))
        new_carry = hend[0:1, :]
    else:
        cin = jnp.where(rows == 0, carry, pltpu.roll(hend, 1, 0))
        new_carry = hend[nseg - 1:nseg, :]
    hs = []
    for c, (acc_a, acc_u) in enumerate(scans):
        cc = cin[c * SUBLANES:(c + 1) * SUBLANES, :]
        hs += [acc_u[j] + acc_a[j] * cc for j in range(SEG)]
    return jnp.concatenate(hs, axis=0), new_carry


def _time_consts():
    r = lax.broadcasted_iota(jnp.int32, (CHUNK, CHUNK), 0)
    c = lax.broadcasted_iota(jnp.int32, (CHUNK, CHUNK), 1)
    tr = (r % SUBLANES) * SEG + r // SUBLANES
    tc = (c % SUBLANES) * SEG + c // SUBLANES
    return tr, tc


def _expand_mat(first_row):
    r = lax.broadcasted_iota(jnp.int32, (LANES, D_GROUP), 0)
    c = lax.broadcasted_iota(jnp.int32, (LANES, D_GROUP), 1)
    return jnp.where(r == first_row + c // SSD_HEAD_DIM, 1.0, 0.0)


def _ret_gamma_log(h):
    return math.log1p(-(2.0 ** (-5.0 - h)))


def _neg_exp_alog(alog_ref):
    lane = lax.broadcasted_iota(jnp.int32, (1, LANES), 1)
    return jnp.where(lane < 2 * SSD_HEADS, -jnp.exp(alog_ref[...]), 0.0)


def _ret_consts(rc_ref, tr, tc):
    dist = jnp.abs(tr - tc).astype(F32)
    pos = tr.astype(F32)
    for hd in range(RET_HEADS):
        lg = _ret_gamma_log(hd)
        rc_ref[hd] = jnp.exp(lg * dist)
        rc_ref[RET_HEADS + hd] = jnp.exp(lg * (pos + 1.0))
        rc_ref[2 * RET_HEADS + hd] = jnp.exp(lg * (CHUNK - 1.0 - pos))
        rc_ref[3 * RET_HEADS + hd] = jnp.exp(lg * (CHUNK - pos))
        rc_ref[4 * RET_HEADS + hd] = jnp.exp(lg * pos)


def _retention_scores(act_ref, rc_ref, rows):
    zeros_h = jnp.zeros((CHUNK, RET_HEAD_DIM), BF16)
    scs = []
    kvs = []
    for hp in range(RET_HEADS // 2):
        lo = 2 * hp * RET_HEAD_DIM
        q2 = act_ref[rows, Q_OFF + lo:Q_OFF + lo + 2 * RET_HEAD_DIM]
        k2 = act_ref[rows, K_OFF + lo:K_OFF + lo + 2 * RET_HEAD_DIM]
        v2 = act_ref[rows, V_OFF + lo:V_OFF + lo + 2 * RET_HEAD_DIM]
        ka, kb = k2[:, 0:RET_HEAD_DIM], k2[:, RET_HEAD_DIM:]
        kdiag = jnp.concatenate([jnp.concatenate([ka, zeros_h], axis=1),
                                 jnp.concatenate([zeros_h, kb], axis=1)], axis=0)
        sc = lax.dot_general(q2, kdiag, _NT, preferred_element_type=F32)
        dec = jnp.concatenate([rc_ref[2 * hp], rc_ref[2 * hp + 1]], axis=1)
        scs.append((sc * dec).astype(BF16))
        for j, kk in enumerate((ka, kb)):
            kd = (kk.astype(F32) * rc_ref[2 * RET_HEADS + 2 * hp + j]).astype(BF16)
            kvs.append(lax.dot_general(kd, v2[:, j * RET_HEAD_DIM:(j + 1) * RET_HEAD_DIM], _TN,
                                       preferred_element_type=F32))
    return scs, kvs


def _retention_values(act_ref, rows, scs):
    zeros_h = jnp.zeros((CHUNK, RET_HEAD_DIM), BF16)
    ys = []
    for hp in range(RET_HEADS // 2):
        lo = 2 * hp * RET_HEAD_DIM
        v2 = act_ref[rows, V_OFF + lo:V_OFF + lo + 2 * RET_HEAD_DIM]
        vdiag = jnp.concatenate([jnp.concatenate([v2[:, 0:RET_HEAD_DIM], zeros_h], axis=1),
                                 jnp.concatenate([zeros_h, v2[:, RET_HEAD_DIM:]], axis=1)], axis=0)
        ys.append(_dot(scs[hp], vdiag))
    return ys


def _mixf_body(act_ref, prev_ref, next_ref, dtr_ref, cw_ref, cb_ref, wg_ref, bg_ref, lam_ref,
               dtb_ref, alog_ref, dsk_ref, part_ref, cv_ref,
               xbc_ref, dts_ref, hc_ref, sf_ref, rf_ref, rc_ref):
    i = pl.program_id(1)
    n = pl.num_programs(1)
    tb = act_ref.shape[0]
    tr, tc = _time_consts()

    @pl.when(i == 0)
    def _():
        hc_ref[...] = jnp.zeros(hc_ref.shape, F32)
        sf_ref[...] = jnp.zeros(sf_ref.shape, F32)
        rf_ref[...] = jnp.zeros(rf_ref.shape, F32)
        _ret_consts(rc_ref, tr, tc)

    nck = tb // CHUNK
    rows = [slice(c * CHUNK, (c + 1) * CHUNK) for c in range(nck)]
    lanes = rows

    ret_sc = []
    ret_kv = []
    for c in range(nck):
        scs, kvs = _retention_scores(act_ref, rc_ref, rows[c])
        ret_sc.append(scs)
        ret_kv.append(kvs)

    prev = prev_ref[...].astype(F32) * jnp.where(i == 0, 0.0, 1.0)
    nxt = next_ref[...].astype(F32) * jnp.where(i == n - 1, 0.0, 1.0)
    raw = act_ref[:, 0:N_CONV].astype(F32)
    n1, p1, p2 = _conv_edges(raw, prev, nxt)
    xcs = []
    for c in range(nck):
        seg = slice(c * SUBLANES, (c + 1) * SUBLANES)
        conv = _conv_chunk(raw[rows[c], :], n1[seg, :], p1[seg, :], p2[seg, :], cw_ref, cb_ref)
        xbc = _silu(conv[:, D_GROUP:N_CONV])
        xbc_ref[rows[c], :] = xbc
        cv_ref[rows[c], 0:D_GROUP] = conv[:, 0:D_GROUP].astype(BF16)
        cv_ref[rows[c], D_GROUP:N_CONV] = xbc.astype(BF16)
        xcs.append(conv[:, 0:D_GROUP])
    xc = jnp.concatenate(xcs, axis=0)

    r_pre, i_pre = _lru_gate_matmul(xc, wg_ref)
    sp = _softplus(-lam_ref[...])
    dts_ref[...] = _softplus(dtr_ref[...] + dtb_ref[...])

    lower = tr > tc
    diag = tr == tc
    ltri = jnp.where(tr >= tc, 1.0, 0.0)
    ltri3 = jnp.concatenate([ltri, ltri, ltri], axis=1).astype(BF16)
    utri = jnp.where(tr <= tc, 1.0, 0.0)
    expand = _expand_mat(0)
    a_row = _neg_exp_alog(alog_ref)
    last = CHUNK - 1
    lane = lax.broadcasted_iota(jnp.int32, (CHUNK, LANES), 1)
    first_head = lane < SSD_HEAD_DIM

    dt_all = jnp.concatenate([dts_ref[r, :] for r in rows], axis=1)
    da_all = dt_all * jnp.concatenate([a_row] * nck, axis=1)
    cum_all = _cumsum_rows(ltri3, da_all)
    tot_all = cum_all[last:last + 1, :]
    rcum_all = tot_all - cum_all + da_all
    da_t = jnp.concatenate([da_all[:, l].T[0:BF16_SUBLANES, :] for l in lanes], axis=0)
    dt_t = jnp.concatenate([dt_all[:, l].T[0:BF16_SUBLANES, :] for l in lanes], axis=0)
    cum_t = _dot_hi(da_t, utri)
    rcum_t = cum_t[:, last:last + 1] - cum_t + da_t
    ldt = jnp.log(dt_t)
    sub_f = cum_t - ldt
    sub_b = rcum_t - ldt
    e_rows = []
    for l in lanes:
        e_rows += [jnp.exp(tot_all[:, l] - cum_all[:, l]) * dt_all[:, l], jnp.exp(cum_all[:, l])]
    e2 = _dot(jnp.concatenate(e_rows, axis=0).astype(BF16), expand.astype(BF16))
    cdec = _dot_hi(jnp.concatenate([jnp.broadcast_to(jnp.exp(tot_all[:, l]), (SUBLANES, LANES)) for l in lanes],
                                   axis=0), expand)

    cbs = []
    st_all = []
    for c in range(nck):
        x = xbc_ref[rows[c], 0:D_GROUP]
        bm = xbc_ref[rows[c], D_GROUP:D_GROUP + LANES].astype(BF16)
        cm = xbc_ref[rows[c], D_GROUP + LANES:SSD_XBC].astype(BF16)
        xd = (x * e2[2 * c * CHUNK:(2 * c + 1) * CHUNK, :]).astype(BF16)
        cbs.append([lax.dot_general(cm[:, g * SSD_STATE:(g + 1) * SSD_STATE], bm[:, g * SSD_STATE:(g + 1) * SSD_STATE],
                                    _NT, preferred_element_type=F32) for g in range(SSD_GROUPS)])
        st_all.append([lax.dot_general(bm[:, g * SSD_STATE:(g + 1) * SSD_STATE], xd[:, g * 256:(g + 1) * 256],
                                       _TN, preferred_element_type=F32) for g in range(SSD_GROUPS)])

    ret_intra = [_retention_values(act_ref, rows[c], ret_sc[c]) for c in range(nck)]
    y_ssd = []
    scans = []
    for c in range(nck):
        x = xbc_ref[rows[c], 0:D_GROUP]
        xb = x.astype(BF16)
        cum = cum_all[:, lanes[c]]
        rcum = rcum_all[:, lanes[c]]
        ldiag = jnp.log(dt_t[16 * c:16 * c + SSD_HEADS, :] + dt_t[16 * c + SSD_HEADS:16 * (c + 1), :])
        ys = []
        for g in range(SSD_GROUPS):
            for pp in range(SSD_HPG // 2):
                ws = []
                for hd in (g * SSD_HPG + 2 * pp, g * SSD_HPG + 2 * pp + 1):
                    hb = SSD_HEADS + hd
                    rf = 16 * c + hd
                    rb = 16 * c + hb
                    z = jnp.where(lower, cum[:, hd:hd + 1] - sub_f[rf:rf + 1, :],
                                  jnp.where(diag, ldiag[hd:hd + 1, :], rcum[:, hb:hb + 1] - sub_b[rb:rb + 1, :]))
                    ws.append((cbs[c][g] * jnp.exp(z)).astype(BF16))
                p0 = (g * SSD_HPG + 2 * pp) * SSD_HEAD_DIM
                xp = xb[:, p0:p0 + LANES]
                zero = jnp.zeros_like(xp)
                xdiag = jnp.concatenate([jnp.where(first_head, xp, zero), jnp.where(first_head, zero, xp)], axis=0)
                ys.append(_dot(jnp.concatenate(ws, axis=1), xdiag))
        y_ssd.append(jnp.concatenate(ys, axis=1) + dsk_ref[...] * x)
        a, u = _lru_coeffs(xc[rows[c], :], r_pre[rows[c], :], i_pre[rows[c], :], bg_ref, sp)
        scans.append(_seg_scan(a, u, reverse=False))

    h, hc_ref[...] = _lru_link(scans, hc_ref[...], reverse=False)
    part_ref[:, 0:D_GROUP] = h

    for c in range(nck):
        cm = xbc_ref[rows[c], D_GROUP + LANES:SSD_XBC].astype(BF16)
        ecum_x = e2[(2 * c + 1) * CHUNK:(2 * c + 2) * CHUNK, :]
        cdec_x = cdec[SUBLANES * c:SUBLANES * c + 1, :]
        yo = [_dot(cm[:, g * SSD_STATE:(g + 1) * SSD_STATE], sf_ref[g].astype(BF16)) for g in range(SSD_GROUPS)]
        inter = [_dot(act_ref[rows[c], Q_OFF + hd * RET_HEAD_DIM:Q_OFF + (hd + 1) * RET_HEAD_DIM],
                      rf_ref[hd].astype(BF16)) for hd in range(RET_HEADS)]
        for g in range(SSD_GROUPS):
            sf_ref[g] = sf_ref[g] * cdec_x[:, g * 256:(g + 1) * 256] + st_all[c][g]
        for hd in range(RET_HEADS):
            rf_ref[hd] = rf_ref[hd] * math.exp(_ret_gamma_log(hd) * CHUNK) + ret_kv[c][hd]
        part_ref[rows[c], D_GROUP:2 * D_GROUP] = y_ssd[c] + jnp.concatenate(yo, axis=1) * ecum_x
        part_ref[rows[c], 2 * D_GROUP:3 * D_GROUP] = (
            jnp.concatenate(ret_intra[c], axis=1)
            + jnp.concatenate([inter[hd] * rc_ref[RET_HEADS + hd] for hd in range(RET_HEADS)], axis=1))


def _mix_fwd(act, dtr, cw, cb, wg, bg, lam, dtb, alog, dsk):
    b, s, _ = act.shape
    tb = min(MIX_TOKENS, s)
    per = tb // HALO
    last = s // HALO - 1
    const = lambda bi, i: (0, 0)
    row = lambda bi, i: (bi, i, 0)
    return pl.pallas_call(
        _mixf_body,
        out_shape=(jax.ShapeDtypeStruct((b, s, N_PART), F32),
                   jax.ShapeDtypeStruct((b, s, N_CONV), BF16)),
        grid=(b, s // tb),
        in_specs=[
            pl.BlockSpec((None, tb, N_ACT), row),
            pl.BlockSpec((None, HALO, N_CONV), lambda bi, i: (bi, jnp.maximum(i * per - 1, 0), 0)),
            pl.BlockSpec((None, HALO, N_CONV), lambda bi, i: (bi, jnp.minimum((i + 1) * per, last), 0)),
            pl.BlockSpec((None, tb, LANES), row),
            pl.BlockSpec((CONV_W, N_CONV), const),
            pl.BlockSpec((1, N_CONV), const),
            pl.BlockSpec((D_GROUP // LANES, LANES, 2 * LANES), lambda bi, i: (0, 0, 0)),
            pl.BlockSpec((1, 2 * D_GROUP), const),
            pl.BlockSpec((1, D_GROUP), const),
            pl.BlockSpec((1, LANES), const),
            pl.BlockSpec((1, LANES), const),
            pl.BlockSpec((1, D_GROUP), const),
        ],
        out_specs=(pl.BlockSpec((None, tb, N_PART), row),
                   pl.BlockSpec((None, tb, N_CONV), row)),
        scratch_shapes=[
            pltpu.VMEM((tb, SSD_XBC), F32),
            pltpu.VMEM((tb, LANES), F32),
            pltpu.VMEM((1, D_GROUP), F32),
            pltpu.VMEM((SSD_GROUPS, SSD_STATE, SSD_HPG * SSD_HEAD_DIM), F32),
            pltpu.VMEM((RET_HEADS, RET_HEAD_DIM, RET_HEAD_DIM), F32),
            pltpu.VMEM((5 * RET_HEADS, CHUNK, CHUNK), F32),
        ],
        compiler_params=pltpu.CompilerParams(
            dimension_semantics=("arbitrary", "arbitrary"), vmem_limit_bytes=VMEM_LIMIT),
        name="mix_fwd",
    )(act, act, act, dtr, cw, cb, wg, bg, lam, dtb, alog, dsk)


def _mixb_body(x_ref, act_ref, cv_ref, gate_ref, dtr_ref, part_ref,
               wg_ref, bg_ref, lam_ref, dtb_ref, alog_ref,
               snw_ref, rnw_ref, wo_ref, o_ref,
               dts_ref, hc_ref, sb_ref, rb_ref, rc_ref):
    i = pl.program_id(1)
    tb = act_ref.shape[0]
    tr, tc = _time_consts()
    nck = tb // CHUNK
    rows = [slice(c * CHUNK, (c + 1) * CHUNK) for c in range(nck)]
    lanes = rows

    @pl.when(i == 0)
    def _():
        hc_ref[...] = jnp.zeros(hc_ref.shape, F32)
        sb_ref[...] = jnp.zeros(sb_ref.shape, F32)
        rb_ref[...] = jnp.zeros(rb_ref.shape, F32)
        _ret_consts(rc_ref, tr, tc)

    xc = cv_ref[:, 0:D_GROUP].astype(F32)
    r_pre, i_pre = _lru_gate_matmul(xc, wg_ref)
    kv_all = []
    for c in range(nck):
        kvs = []
        for hd in range(RET_HEADS):
            k = act_ref[rows[c], K_OFF + hd * RET_HEAD_DIM:K_OFF + (hd + 1) * RET_HEAD_DIM]
            v = act_ref[rows[c], V_OFF + hd * RET_HEAD_DIM:V_OFF + (hd + 1) * RET_HEAD_DIM]
            kd = (k.astype(F32) * rc_ref[4 * RET_HEADS + hd]).astype(BF16)
            kvs.append(lax.dot_general(kd, v, _TN, preferred_element_type=F32))
        kv_all.append(kvs)
    dts_ref[...] = _softplus(dtr_ref[...] + dtb_ref[...])
    ltri = jnp.where(tr >= tc, 1.0, 0.0)
    ltri3 = jnp.concatenate([ltri, ltri, ltri], axis=1).astype(BF16)
    expand = _expand_mat(SSD_HEADS)
    a_row = _neg_exp_alog(alog_ref)
    last = CHUNK - 1
    dt_all = jnp.concatenate([dts_ref[r, :] for r in rows], axis=1)
    da_all = dt_all * jnp.concatenate([a_row] * nck, axis=1)
    cum_all = _cumsum_rows(ltri3, da_all)
    tot_all = cum_all[last:last + 1, :]
    rcum_all = tot_all - cum_all + da_all
    e_rows = []
    for l in lanes:
        e_rows += [jnp.exp(tot_all[:, l] - rcum_all[:, l]) * dt_all[:, l], jnp.exp(rcum_all[:, l])]
    e2 = _dot(jnp.concatenate(e_rows, axis=0).astype(BF16), expand.astype(BF16))
    cdec = _dot_hi(jnp.concatenate([jnp.broadcast_to(jnp.exp(tot_all[:, l]), (SUBLANES, LANES)) for l in lanes],
                                   axis=0), expand)
    st_all = []
    for c in range(nck):
        xb = cv_ref[rows[c], D_GROUP:2 * D_GROUP]
        bm = cv_ref[rows[c], 2 * D_GROUP:2 * D_GROUP + LANES]
        xd = (xb.astype(F32) * e2[2 * c * CHUNK:(2 * c + 1) * CHUNK, :]).astype(BF16)
        st_all.append([lax.dot_general(bm[:, g * SSD_STATE:(g + 1) * SSD_STATE], xd[:, g * 256:(g + 1) * 256],
                                       _TN, preferred_element_type=F32) for g in range(SSD_GROUPS)])

    sp = _softplus(-lam_ref[...])
    scans = [None] * nck
    y_ret = [None] * nck
    y_ssd = [None] * nck
    for c in range(nck - 1, -1, -1):
        cm = cv_ref[rows[c], 2 * D_GROUP + LANES:N_CONV]
        yo = [_dot(cm[:, g * SSD_STATE:(g + 1) * SSD_STATE], sb_ref[g].astype(BF16)) for g in range(SSD_GROUPS)]
        inter = [_dot(act_ref[rows[c], Q_OFF + hd * RET_HEAD_DIM:Q_OFF + (hd + 1) * RET_HEAD_DIM],
                      rb_ref[hd].astype(BF16)) for hd in range(RET_HEADS)]
        cdec_x = cdec[SUBLANES * c:SUBLANES * c + 1, :]
        for g in range(SSD_GROUPS):
            sb_ref[g] = sb_ref[g] * cdec_x[:, g * 256:(g + 1) * 256] + st_all[c][g]
        for hd in range(RET_HEADS):
            rb_ref[hd] = rb_ref[hd] * math.exp(_ret_gamma_log(hd) * CHUNK) + kv_all[c][hd]

        a, u = _lru_coeffs(xc[rows[c], :], r_pre[rows[c], :], i_pre[rows[c], :], bg_ref, sp)
        scans[c] = _seg_scan(a, u, reverse=True)

        ecum_x = e2[(2 * c + 1) * CHUNK:(2 * c + 2) * CHUNK, :]
        ys = part_ref[rows[c], D_GROUP:2 * D_GROUP] + jnp.concatenate(yo, axis=1) * ecum_x
        ys = ys * _silu(gate_ref[rows[c], D_GROUP:2 * D_GROUP].astype(F32))
        ys = ys * lax.rsqrt(jnp.mean(ys * ys, axis=-1, keepdims=True) + EPS) * snw_ref[...]
        y_ssd[c] = ys.astype(BF16)
        yr = []
        for hd in range(RET_HEADS):
            cs = slice(2 * D_GROUP + hd * RET_HEAD_DIM, 2 * D_GROUP + (hd + 1) * RET_HEAD_DIM)
            y = part_ref[rows[c], cs] + inter[hd] * rc_ref[3 * RET_HEADS + hd]
            mu = jnp.mean(y, axis=-1, keepdims=True)
            yc = y - mu
            var = jnp.mean(yc * yc, axis=-1, keepdims=True)
            ws = slice(hd * RET_HEAD_DIM, (hd + 1) * RET_HEAD_DIM)
            yr.append(yc * lax.rsqrt(var + EPS) * rnw_ref[:, ws] * _silu(gate_ref[rows[c], cs].astype(F32)))
        y_ret[c] = jnp.concatenate(yr, axis=1).astype(BF16)

    acc = _dot(jnp.concatenate(y_ret, axis=0), wo_ref[2 * D_GROUP:3 * D_GROUP, :])
    acc = acc + _dot(jnp.concatenate(y_ssd, axis=0), wo_ref[D_GROUP:2 * D_GROUP, :])
    h, hc_ref[...] = _lru_link(scans, hc_ref[...], reverse=True)
    y_lru = (part_ref[:, 0:D_GROUP] + h) * _gelu_tanh(gate_ref[:, 0:D_GROUP].astype(F32))
    o_ref[...] = x_ref[...] + (acc + _dot(y_lru.astype(BF16), wo_ref[0:D_GROUP, :]))


def _mix_bwd(x, act, cv, gate, dtr, part, wg, bg, lam, dtb, alog, snw, rnw, wo):
    b, s, _ = act.shape
    tb = min(MIX_TOKENS, s)
    nb = s // tb
    const = lambda bi, i: (0, 0)
    row = lambda bi, i: (bi, nb - 1 - i, 0)
    return pl.pallas_call(
        _mixb_body,
        out_shape=jax.ShapeDtypeStruct((b, s, D_MODEL), F32),
        grid=(b, nb),
        in_specs=[
            pl.BlockSpec((None, tb, D_MODEL), row),
            pl.BlockSpec((None, tb, N_ACT), row),
            pl.BlockSpec((None, tb, N_CONV), row),
            pl.BlockSpec((None, tb, N_GATE), row),
            pl.BlockSpec((None, tb, LANES), row),
            pl.BlockSpec((None, tb, N_PART), row),
            pl.BlockSpec((D_GROUP // LANES, LANES, 2 * LANES), lambda bi, i: (0, 0, 0)),
            pl.BlockSpec((1, 2 * D_GROUP), const),
            pl.BlockSpec((1, D_GROUP), const),
            pl.BlockSpec((1, LANES), const),
            pl.BlockSpec((1, LANES), const),
            pl.BlockSpec((1, D_GROUP), const),
            pl.BlockSpec((1, D_GROUP), const),
            pl.BlockSpec((N_PART, D_MODEL), const),
        ],
        out_specs=pl.BlockSpec((None, tb, D_MODEL), row),
        scratch_shapes=[
            pltpu.VMEM((tb, LANES), F32),
            pltpu.VMEM((1, D_GROUP), F32),
            pltpu.VMEM((SSD_GROUPS, SSD_STATE, SSD_HPG * SSD_HEAD_DIM), F32),
            pltpu.VMEM((RET_HEADS, RET_HEAD_DIM, RET_HEAD_DIM), F32),
            pltpu.VMEM((5 * RET_HEADS, CHUNK, CHUNK), F32),
        ],
        compiler_params=pltpu.CompilerParams(
            dimension_semantics=("arbitrary", "arbitrary"), vmem_limit_bytes=VMEM_LIMIT),
        name="mix_bwd",
    )(x, act, cv, gate, dtr, part, wg, bg, lam, dtb, alog, snw, rnw, wo)


def _norm_body(x_ref, w_ref, o_ref):
    o_ref[...] = _rms(x_ref[...], w_ref[...])


def _final_norm(x, w):
    t = x.shape[0]
    tm = min(1024, t)
    return pl.pallas_call(
        _norm_body,
        out_shape=jax.ShapeDtypeStruct(x.shape, F32),
        grid=(t // tm,),
        in_specs=[pl.BlockSpec((tm, D_MODEL), lambda i: (i, 0)),
                  pl.BlockSpec((1, D_MODEL), lambda i: (0, 0))],
        out_specs=pl.BlockSpec((tm, D_MODEL), lambda i: (i, 0)),
        compiler_params=pltpu.CompilerParams(dimension_semantics=("arbitrary",)),
        name="final_norm",
    )(x, w)


def _to_chunk_order(x):
    b, s, d = x.shape
    return x.reshape(b, s // CHUNK, SUBLANES, SEG, d).swapaxes(2, 3).reshape(b, s, d)


def _from_chunk_order(x):
    b, s, d = x.shape
    return x.reshape(b, s // CHUNK, SEG, SUBLANES, d).swapaxes(2, 3).reshape(b, s, d)


def _block_diag(w):
    nb, bw, _ = w.shape
    eye = jnp.eye(nb, dtype=w.dtype)
    return (eye[:, None, :, None] * w[:, :, None, :]).reshape(nb * bw, nb * bw)


def _pad_lanes(v):
    return jnp.pad(v, (0, LANES - v.shape[0]))[None, :]


def _prep_layer(l, p):
    w_in = p['w_in'][l]
    o = [0, 512, 1024, 1536, 2304, 2320, 2832, 3344, 3856, 4368]
    lru_x, lru_gate, ssd_z, xbc, dtc, q, k, v, g = [w_in[:, o[j]:o[j + 1]] for j in range(9)]
    gate_w = []
    gate_b = []
    for d in range(2):
        wa = _block_diag(p['lru_w_a'][l, d])
        wi = _block_diag(p['lru_w_i'][l, d])
        gate_w.append(jnp.stack([jnp.concatenate([w[j:j + LANES, j:j + LANES] for w in (wa, wi)], axis=1)
                                 for j in range(0, D_GROUP, LANES)]))
        gate_b.append(jnp.concatenate([p['lru_b_a'][l, d], p['lru_b_i'][l, d]])[None, :])
    return dict(
        ffn1_nw=p['ffn1_norm'][l][None, :], ffn1_wgu=p['ffn1_w_gu'][l].astype(BF16), ffn1_wd=p['ffn1_w_down'][l].astype(BF16),
        ffn2_nw=p['ffn2_norm'][l][None, :], ffn2_wgu=p['ffn2_w_gu'][l].astype(BF16), ffn2_wd=p['ffn2_w_down'][l].astype(BF16),
        mix_nw=p['mix_norm'][l][None, :],
        wa=jnp.concatenate([lru_x, xbc, q, k, v], axis=1).astype(BF16),
        wg=jnp.concatenate([lru_gate, ssd_z, g], axis=1).astype(BF16),
        wdt=jnp.pad(dtc, ((0, 0), (0, LANES - dtc.shape[1]))).astype(BF16),
        cw=jnp.concatenate([p['lru_conv_w'][l], p['ssd_conv_w'][l]], axis=1),
        cb=jnp.concatenate([p['lru_conv_b'][l], p['ssd_conv_b'][l]])[None, :],
        gate_w=[w.astype(BF16) for w in gate_w], gate_b=gate_b,
        lam=[p['lru_lam'][l, d][None, :] for d in range(2)],
        dtb=_pad_lanes(p['ssd_dt_bias'][l].reshape(-1)),
        alog=_pad_lanes(p['ssd_a_log'][l].reshape(-1)),
        dsk=jnp.repeat(p['ssd_d'][l], SSD_HEAD_DIM)[None, :],
        snw=p['ssd_norm'][l][None, :], rnw=p['ret_norm'][l][None, :],
        wo=p['w_out'][l].astype(BF16),
    )


def _rope_tables(s):
    d = RET_HEAD_DIM
    inv_freq = 1.0 / (ROPE_BASE ** (jnp.arange(0, d, 2, dtype=F32) / d))
    r = jnp.arange(s, dtype=jnp.int32)
    local = r % CHUNK
    pos = (r // CHUNK) * CHUNK + (local % SUBLANES) * SEG + local // SUBLANES
    ang = pos.astype(F32)[:, None] * inv_freq[None, :]
    cos = jnp.cos(ang)
    sin = jnp.sin(ang)
    return jnp.concatenate([cos, cos], axis=1), jnp.concatenate([-sin, sin], axis=1)


def _trunk(x, layers, final_nw):
    b, s, d = x.shape
    c2, s2 = _rope_tables(s)
    x = _to_chunk_order(x)
    for lp in layers:
        x = _ffn(x.reshape(b * s, d), lp['ffn1_nw'], lp['ffn1_wgu'], lp['ffn1_wd']).reshape(b, s, d)
        act, gate, dtr = _proj(x, lp['mix_nw'], lp['wa'], lp['wg'], lp['wdt'], c2, s2)
        part, cv = _mix_fwd(act, dtr, lp['cw'], lp['cb'], lp['gate_w'][0], lp['gate_b'][0], lp['lam'][0],
                            lp['dtb'], lp['alog'], lp['dsk'])
        x = _mix_bwd(x, act, cv, gate, dtr, part, lp['gate_w'][1], lp['gate_b'][1], lp['lam'][1],
                     lp['dtb'], lp['alog'], lp['snw'], lp['rnw'], lp['wo'])
        x = _ffn(x.reshape(b * s, d), lp['ffn2_nw'], lp['ffn2_wgu'], lp['ffn2_wd']).reshape(b, s, d)
    return _from_chunk_order(_final_norm(x.reshape(b * s, d), final_nw).reshape(b, s, d))


def kernel(x_prompt, x_sample, ffn1_norm, ffn1_w_gu, ffn1_w_down, mix_norm, w_in, lru_conv_w, lru_conv_b, lru_w_a, lru_b_a, lru_w_i, lru_b_i, lru_lam, ssd_conv_w, ssd_conv_b, ssd_dt_bias, ssd_a_log, ssd_d, ssd_norm, ret_norm, w_out, ffn2_norm, ffn2_w_gu, ffn2_w_down, final_norm):
    p = dict(ffn1_norm=ffn1_norm, ffn1_w_gu=ffn1_w_gu, ffn1_w_down=ffn1_w_down, mix_norm=mix_norm, w_in=w_in,
             lru_conv_w=lru_conv_w, lru_conv_b=lru_conv_b, lru_w_a=lru_w_a, lru_b_a=lru_b_a, lru_w_i=lru_w_i,
             lru_b_i=lru_b_i, lru_lam=lru_lam, ssd_conv_w=ssd_conv_w, ssd_conv_b=ssd_conv_b, ssd_dt_bias=ssd_dt_bias,
             ssd_a_log=ssd_a_log, ssd_d=ssd_d, ssd_norm=ssd_norm, ret_norm=ret_norm, w_out=w_out,
             ffn2_norm=ffn2_norm, ffn2_w_gu=ffn2_w_gu, ffn2_w_down=ffn2_w_down)
    layers = [_prep_layer(l, p) for l in range(ffn1_norm.shape[0])]
    final_nw = final_norm[None, :]
    return (_trunk(x_prompt, layers, final_nw), _trunk(x_sample, layers, final_nw))
```

```python
import math

import jax
import jax.numpy as jnp
from jax import lax
from jax.experimental import pallas as pl
from jax.experimental.pallas import tpu as pltpu

F32 = jnp.float32
BF16 = jnp.bfloat16

D_MODEL = 1024
D_FF = 2816
D_GROUP = 512
EPS = 1e-6
CONV_W = 4
LRU_C = 8.0
SSD_HEADS = 8
SSD_HEAD_DIM = 64
SSD_STATE = 64
SSD_GROUPS = 2
SSD_HPG = SSD_HEADS // SSD_GROUPS
SSD_XBC = D_GROUP + 2 * SSD_GROUPS * SSD_STATE
RET_HEADS = 4
RET_HEAD_DIM = 128
CHUNK = 128
ROPE_BASE = 10000.0

LANES = 128
SUBLANES = 8
BF16_SUBLANES = 16
SEG = CHUNK // SUBLANES
HALO = BF16_SUBLANES
VMEM_LIMIT = 56 * 1024 * 1024

N_CONV = D_GROUP + SSD_XBC
N_QKV = 3 * D_GROUP
N_GATE = 3 * D_GROUP
Q_OFF = 0
K_OFF = D_GROUP
V_OFF = 2 * D_GROUP
N_PART = 3 * D_GROUP

FFN_TOKENS = 1024
PROJ_TOKENS = 1024
MIX_TOKENS = 512
FF_CHUNK = 512
TINY = 1e-30

_NT = (((1,), (1,)), ((), ()))
_TN = (((0,), (0,)), ((), ()))


def _dot(a, b):
    return jnp.dot(a, b, preferred_element_type=F32)


def _dot_hi(a, b):
    return jnp.dot(a, b, preferred_element_type=F32, precision=lax.Precision.HIGHEST)


def _rms(x, w):
    return x * lax.rsqrt(jnp.mean(x * x, axis=-1, keepdims=True) + EPS) * w


def _sigmoid(x):
    return 0.5 * jnp.tanh(0.5 * x) + 0.5


def _silu(x):
    h = 0.5 * x
    return h * jnp.tanh(h) + h


def _cumsum_rows(tri3, x):
    hi = x.astype(BF16)
    r1 = x - hi.astype(F32)
    mid = r1.astype(BF16)
    lo = (r1 - mid.astype(F32)).astype(BF16)
    return _dot(tri3, jnp.concatenate([hi, mid, lo], axis=0))


def _softplus(x):
    return jnp.maximum(x, 0.0) + jnp.log(1.0 + jnp.exp(-jnp.abs(x)))


def _gelu_tanh(x):
    c = math.sqrt(2.0 / math.pi)
    return 0.5 * x * (1.0 + jnp.tanh(c * (x + 0.044715 * (x * x * x))))


def _ffn_body(x_ref, nw_ref, wgu_ref, wd_ref, *rest):
    o_ref = rest[-1]
    x = x_ref[...]
    xb = _rms(x, nw_ref[...]).astype(BF16)
    acc = jnp.zeros(x.shape, F32)
    for c in range(0, D_FF, FF_CHUNK):
        w = min(FF_CHUNK, D_FF - c)
        g = _dot(xb, wgu_ref[:, c:c + w])
        u = _dot(xb, wgu_ref[:, D_FF + c:D_FF + c + w])
        a = (_silu(g) * u).astype(BF16)
        acc = acc + _dot(a, wd_ref[c:c + w, :])
    y = x + 0.5 * acc
    o_ref[...] = _rms(y, rest[0][...]) if len(rest) == 2 else y


def _ffn_weight_specs():
    const = lambda *_: (0, 0)
    return [
        pl.BlockSpec((1, D_MODEL), const),
        pl.BlockSpec((D_MODEL, 2 * D_FF), const, pipeline_mode=pl.Buffered(1)),
        pl.BlockSpec((D_FF, D_MODEL), const, pipeline_mode=pl.Buffered(1)),
    ]


def _ffn(x, nw, wgu, wd, final_nw=None):
    t = x.shape[0]
    tm = min(FFN_TOKENS, t)
    extra = [] if final_nw is None else [final_nw]
    return pl.pallas_call(
        _ffn_body,
        out_shape=jax.ShapeDtypeStruct(x.shape, F32),
        grid=(t // tm,),
        in_specs=[pl.BlockSpec((tm, D_MODEL), lambda i: (i, 0))] + _ffn_weight_specs()
        + [pl.BlockSpec((1, D_MODEL), lambda i: (0, 0))] * len(extra),
        out_specs=pl.BlockSpec((tm, D_MODEL), lambda i: (i, 0)),
        compiler_params=pltpu.CompilerParams(
            dimension_semantics=("arbitrary",), vmem_limit_bytes=VMEM_LIMIT),
        name="ffn",
    )(x, nw, wgu, wd, *extra)


def _proj_body(x_ref, xp_ref, xn_ref, nw_ref, wc_ref, wq_ref, wg_ref, wdt_ref, c2_ref, s2_ref, cw_ref, cb_ref,
               cv_ref, qkv_ref, g_ref, dt_ref):
    i = pl.program_id(1)
    n = pl.num_programs(1)
    tm = x_ref.shape[0]
    nw = nw_ref[...]
    xb = _rms(x_ref[...], nw).astype(BF16)
    xp = _rms(xp_ref[...] * jnp.where(i == 0, 0.0, 1.0), nw).astype(BF16)
    xn = _rms(xn_ref[...] * jnp.where(i == n - 1, 0.0, 1.0), nw).astype(BF16)
    pc = _dot(jnp.concatenate([xp, xb, xn], axis=0), wc_ref[...])
    pq = _dot(xb, wq_ref[...])
    g_ref[...] = _dot(xb, wg_ref[...]).astype(BF16)
    dt_ref[...] = _dot(xb, wdt_ref[...])
    raw = pc[HALO:HALO + tm, :]
    n1, p1, p2 = _conv_edges(raw, pc[0:HALO, :], pc[HALO + tm:HALO + tm + HALO, :])
    for c in range(tm // CHUNK):
        rows = slice(c * CHUNK, (c + 1) * CHUNK)
        seg = slice(c * SUBLANES, (c + 1) * SUBLANES)
        conv = _conv_chunk(raw[rows, :], n1[seg, :], p1[seg, :], p2[seg, :], cw_ref, cb_ref)
        cv_ref[rows, 0:D_GROUP] = conv[:, 0:D_GROUP].astype(BF16)
        cv_ref[rows, D_GROUP:N_CONV] = _silu(conv[:, D_GROUP:N_CONV]).astype(BF16)
    qkv_ref[:, V_OFF:N_QKV] = pq[:, V_OFF:N_QKV].astype(BF16)
    c2 = c2_ref[...]
    s2 = s2_ref[...]
    kscale = RET_HEAD_DIM ** -0.5
    for h in range(2 * RET_HEADS):
        lo = h * RET_HEAD_DIM
        blk = pq[:, lo:lo + RET_HEAD_DIM]
        rot = blk * c2 + pltpu.roll(blk, RET_HEAD_DIM // 2, 1) * s2
        if h >= RET_HEADS:
            rot = rot * kscale
        qkv_ref[:, lo:lo + RET_HEAD_DIM] = rot.astype(BF16)


def _proj(x, nw, wc, wq, wg, wdt, c2, s2, cw, cb):
    b, s, _ = x.shape
    tm = min(PROJ_TOKENS, s)
    per = tm // HALO
    last = s // HALO - 1
    const = lambda bi, i: (0, 0)
    row = lambda bi, i: (bi, i, 0)
    return pl.pallas_call(
        _proj_body,
        out_shape=(jax.ShapeDtypeStruct((b, s, N_CONV), BF16),
                   jax.ShapeDtypeStruct((b, s, N_QKV), BF16),
                   jax.ShapeDtypeStruct((b, s, N_GATE), BF16),
                   jax.ShapeDtypeStruct((b, s, LANES), F32)),
        grid=(b, s // tm),
        in_specs=[
            pl.BlockSpec((None, tm, D_MODEL), row),
            pl.BlockSpec((None, HALO, D_MODEL), lambda bi, i: (bi, jnp.maximum(i * per - 1, 0), 0)),
            pl.BlockSpec((None, HALO, D_MODEL), lambda bi, i: (bi, jnp.minimum((i + 1) * per, last), 0)),
            pl.BlockSpec((1, D_MODEL), const),
            pl.BlockSpec((D_MODEL, N_CONV), const, pipeline_mode=pl.Buffered(1)),
            pl.BlockSpec((D_MODEL, N_QKV), const, pipeline_mode=pl.Buffered(1)),
            pl.BlockSpec((D_MODEL, N_GATE), const, pipeline_mode=pl.Buffered(1)),
            pl.BlockSpec((D_MODEL, LANES), const, pipeline_mode=pl.Buffered(1)),
            pl.BlockSpec((tm, LANES), lambda bi, i: (i, 0)),
            pl.BlockSpec((tm, LANES), lambda bi, i: (i, 0)),
            pl.BlockSpec((CONV_W, N_CONV), const),
            pl.BlockSpec((1, N_CONV), const),
        ],
        out_specs=(pl.BlockSpec((None, tm, N_CONV), row),
                   pl.BlockSpec((None, tm, N_QKV), row),
                   pl.BlockSpec((None, tm, N_GATE), row),
                   pl.BlockSpec((None, tm, LANES), row)),
        compiler_params=pltpu.CompilerParams(
            dimension_semantics=("arbitrary", "arbitrary"), vmem_limit_bytes=VMEM_LIMIT),
        name="mix_proj",
    )(x, x, x, nw, wc, wq, wg, wdt, c2, s2, cw, cb)


def _split(x):
    return x.reshape(x.shape[0] // CHUNK, SEG, SUBLANES, x.shape[1])


def _seg_rows(x4, j):
    return x4[:, j].reshape(-1, x4.shape[3])


def _conv_edges(x, prev, nxt):
    x4 = _split(x)
    nck, _, _, ch = x4.shape
    nseg = nck * SUBLANES
    rows = lax.broadcasted_iota(jnp.int32, (nseg, ch), 0)
    n1 = jnp.where(rows == nseg - 1, nxt[0:1, :], pltpu.roll(_seg_rows(x4, 0), nseg - 1, 0))
    p1 = jnp.where(rows == 0, prev[HALO - 1:HALO, :], pltpu.roll(_seg_rows(x4, SEG - 1), 1, 0))
    p2 = jnp.where(rows == 0, prev[HALO - 1 - SUBLANES:HALO - SUBLANES, :],
                   pltpu.roll(_seg_rows(x4, SEG - 2), 1, 0))
    return n1, p1, p2


def _conv_chunk(x, n1, p1, p2, cw_ref, cb_ref):
    x3 = x.reshape(SEG, SUBLANES, x.shape[1])
    xp1 = jnp.concatenate([x3[1:SEG], n1[None]], axis=0)
    xm1 = jnp.concatenate([p1[None], x3[0:SEG - 1]], axis=0)
    xm2 = jnp.concatenate([p2[None], p1[None], x3[0:SEG - 2]], axis=0)
    w = cw_ref[...]
    out = cb_ref[...] + xm2 * w[0:1, :] + xm1 * w[1:2, :] + x3 * w[2:3, :] + xp1 * w[3:4, :]
    return out.reshape(x.shape)


def _lru_gate_matmul(xc, wg_ref):
    xb = xc.astype(BF16)
    rs = []
    gs = []
    for p in range(D_GROUP // LANES):
        g = _dot(xb[:, p * LANES:(p + 1) * LANES], wg_ref[p])
        rs.append(g[:, 0:LANES])
        gs.append(g[:, LANES:2 * LANES])
    return jnp.concatenate(rs, axis=1), jnp.concatenate(gs, axis=1)


def _lru_coeffs(xc, r_pre, i_pre, bg_ref, sp):
    r = _sigmoid(r_pre + bg_ref[:, 0:D_GROUP])
    ig = _sigmoid(i_pre + bg_ref[:, D_GROUP:2 * D_GROUP])
    a = jnp.exp((-LRU_C) * r * sp)
    om = 1.0 - a * a
    u = om * lax.rsqrt(jnp.maximum(om, TINY)) * (ig * xc)
    return a, u


def _row_scan(a, u, reverse):
    n = a.shape[0]
    rows = lax.broadcasted_iota(jnp.int32, a.shape, 0)
    s = 1
    while s < n:
        if reverse:
            a_s = pltpu.roll(a, n - s, 0)
            u_s = pltpu.roll(u, n - s, 0)
            m = rows < n - s
        else:
            a_s = pltpu.roll(a, s, 0)
            u_s = pltpu.roll(u, s, 0)
            m = rows >= s
        u = jnp.where(m, a * u_s + u, u)
        a = jnp.where(m, a * a_s, a)
        s *= 2
    return a, u


def _seg_scan(a, u, reverse):
    a3 = a.reshape(SEG, SUBLANES, a.shape[1])
    u3 = u.reshape(SEG, SUBLANES, u.shape[1])
    acc_a = [None] * SEG
    acc_u = [None] * SEG
    pa = pu = None
    for j in (range(SEG - 1, -1, -1) if reverse else range(SEG)):
        aj = a3[j]
        uj = u3[j]
        if pa is not None:
            uj = aj * pu + uj
            aj = aj * pa
        acc_a[j], acc_u[j] = aj, uj
        pa, pu = aj, uj
    return acc_a, acc_u


def _lru_link(scans, carry, reverse):
    end = 0 if reverse else SEG - 1
    tot_a = jnp.concatenate([sc[0][end] for sc in scans], axis=0)
    tot_u = jnp.concatenate([sc[1][end] for sc in scans], axis=0)
    nseg = tot_a.shape[0]
    pa, hend = _row_scan(tot_a, tot_u, reverse)
    hend = hend + pa * carry
    rows = lax.broadcasted_iota(jnp.int32, hend.shape, 0)
    if reverse:
        cin = jnp.where(rows == nseg - 1, carry, pltpu.roll(hend, nseg - 1, 0))
        new_carry = hend[0:1, :]
    else:
        cin = jnp.where(rows == 0, carry, pltpu.roll(hend, 1, 0))
        new_carry = hend[nseg - 1:nseg, :]
    hs = []
    for c, (acc_a, acc_u) in enumerate(scans):
        cc = cin[c * SUBLANES:(c + 1) * SUBLANES, :]
        hs += [acc_u[j] + acc_a[j] * cc for j in range(SEG)]
    return jnp.concatenate(hs, axis=0), new_carry


def _time_consts():
    r = lax.broadcasted_iota(jnp.int32, (CHUNK, CHUNK), 0)
    c = lax.broadcasted_iota(jnp.int32, (CHUNK, CHUNK), 1)
    tr = (r % SUBLANES) * SEG + r // SUBLANES
    tc = (c % SUBLANES) * SEG + c // SUBLANES
    return tr, tc


def _expand_mat(first_row):
    r = lax.broadcasted_iota(jnp.int32, (LANES, D_GROUP), 0)
    c = lax.broadcasted_iota(jnp.int32, (LANES, D_GROUP), 1)
    return jnp.where(r == first_row + c // SSD_HEAD_DIM, 1.0, 0.0)


def _ret_gamma_log(h):
    return math.log1p(-(2.0 ** (-5.0 - h)))


def _neg_exp_alog(alog_ref):
    lane = lax.broadcasted_iota(jnp.int32, (1, LANES), 1)
    return jnp.where(lane < 2 * SSD_HEADS, -jnp.exp(alog_ref[...]), 0.0)


def _ret_consts(rc_ref, tr, tc):
    dist = jnp.abs(tr - tc).astype(F32)
    pos = tr.astype(F32)
    for hd in range(RET_HEADS):
        lg = _ret_gamma_log(hd)
        rc_ref[hd] = jnp.exp(lg * dist)
        rc_ref[RET_HEADS + hd] = jnp.exp(lg * (pos + 1.0))
        rc_ref[2 * RET_HEADS + hd] = jnp.exp(lg * (CHUNK - 1.0 - pos))
        rc_ref[3 * RET_HEADS + hd] = jnp.exp(lg * (CHUNK - pos))
        rc_ref[4 * RET_HEADS + hd] = jnp.exp(lg * pos)


def _retention_scores(act_ref, rc_ref, rows):
    zeros_h = jnp.zeros((CHUNK, RET_HEAD_DIM), BF16)
    scs = []
    kvs = []
    for hp in range(RET_HEADS // 2):
        lo = 2 * hp * RET_HEAD_DIM
        q2 = act_ref[rows, Q_OFF + lo:Q_OFF + lo + 2 * RET_HEAD_DIM]
        k2 = act_ref[rows, K_OFF + lo:K_OFF + lo + 2 * RET_HEAD_DIM]
        v2 = act_ref[rows, V_OFF + lo:V_OFF + lo + 2 * RET_HEAD_DIM]
        ka, kb = k2[:, 0:RET_HEAD_DIM], k2[:, RET_HEAD_DIM:]
        kdiag = jnp.concatenate([jnp.concatenate([ka, zeros_h], axis=1),
                                 jnp.concatenate([zeros_h, kb], axis=1)], axis=0)
        sc = lax.dot_general(q2, kdiag, _NT, preferred_element_type=F32)
        dec = jnp.concatenate([rc_ref[2 * hp], rc_ref[2 * hp + 1]], axis=1)
        scs.append((sc * dec).astype(BF16))
        for j, kk in enumerate((ka, kb)):
            kd = (kk.astype(F32) * rc_ref[2 * RET_HEADS + 2 * hp + j]).astype(BF16)
            kvs.append(lax.dot_general(kd, v2[:, j * RET_HEAD_DIM:(j + 1) * RET_HEAD_DIM], _TN,
                                       preferred_element_type=F32))
    return scs, kvs


def _retention_values(act_ref, rows, scs):
    zeros_h = jnp.zeros((CHUNK, RET_HEAD_DIM), BF16)
    ys = []
    for hp in range(RET_HEADS // 2):
        lo = 2 * hp * RET_HEAD_DIM
        v2 = act_ref[rows, V_OFF + lo:V_OFF + lo + 2 * RET_HEAD_DIM]
        vdiag = jnp.concatenate([jnp.concatenate([v2[:, 0:RET_HEAD_DIM], zeros_h], axis=1),
                                 jnp.concatenate([zeros_h, v2[:, RET_HEAD_DIM:]], axis=1)], axis=0)
        ys.append(_dot(scs[hp], vdiag))
    return ys


def _mixf_body(cv_ref, act_ref, dtr_ref, wg_ref, bg_ref, lam_ref,
               dtb_ref, alog_ref, dsk_ref, part_ref,
               dts_ref, hc_ref, sf_ref, rf_ref, rc_ref):
    i = pl.program_id(1)
    tb = act_ref.shape[0]
    tr, tc = _time_consts()

    @pl.when(i == 0)
    def _():
        hc_ref[...] = jnp.zeros(hc_ref.shape, F32)
        sf_ref[...] = jnp.zeros(sf_ref.shape, F32)
        rf_ref[...] = jnp.zeros(rf_ref.shape, F32)
        _ret_consts(rc_ref, tr, tc)

    nck = tb // CHUNK
    rows = [slice(c * CHUNK, (c + 1) * CHUNK) for c in range(nck)]
    lanes = rows

    ret_sc = []
    ret_kv = []
    for c in range(nck):
        scs, kvs = _retention_scores(act_ref, rc_ref, rows[c])
        ret_sc.append(scs)
        ret_kv.append(kvs)

    xc = cv_ref[:, 0:D_GROUP].astype(F32)
    r_pre, i_pre = _lru_gate_matmul(xc, wg_ref)
    sp = _softplus(-lam_ref[...])
    dts_ref[...] = _softplus(dtr_ref[...] + dtb_ref[...])

    lower = tr > tc
    diag = tr == tc
    ltri = jnp.where(tr >= tc, 1.0, 0.0)
    ltri3 = jnp.concatenate([ltri, ltri, ltri], axis=1).astype(BF16)
    utri = jnp.where(tr <= tc, 1.0, 0.0)
    expand = _expand_mat(0)
    a_row = _neg_exp_alog(alog_ref)
    last = CHUNK - 1
    lane = lax.broadcasted_iota(jnp.int32, (CHUNK, LANES), 1)
    first_head = lane < SSD_HEAD_DIM

    dt_all = jnp.concatenate([dts_ref[r, :] for r in rows], axis=1)
    da_all = dt_all * jnp.concatenate([a_row] * nck, axis=1)
    cum_all = _cumsum_rows(ltri3, da_all)
    tot_all = cum_all[last:last + 1, :]
    rcum_all = tot_all - cum_all + da_all
    da_t = jnp.concatenate([da_all[:, l].T[0:BF16_SUBLANES, :] for l in lanes], axis=0)
    dt_t = jnp.concatenate([dt_all[:, l].T[0:BF16_SUBLANES, :] for l in lanes], axis=0)
    cum_t = _dot_hi(da_t, utri)
    rcum_t = cum_t[:, last:last + 1] - cum_t + da_t
    ldt = jnp.log(dt_t)
    sub_f = cum_t - ldt
    sub_b = rcum_t - ldt
    e_rows = []
    for l in lanes:
        e_rows += [jnp.exp(tot_all[:, l] - cum_all[:, l]) * dt_all[:, l], jnp.exp(cum_all[:, l])]
    e2 = _dot(jnp.concatenate(e_rows, axis=0).astype(BF16), expand.astype(BF16))
    cdec = _dot_hi(jnp.concatenate([jnp.broadcast_to(jnp.exp(tot_all[:, l]), (SUBLANES, LANES)) for l in lanes],
                                   axis=0), expand)

    cbs = []
    st_all = []
    for c in range(nck):
        x = cv_ref[rows[c], D_GROUP:2 * D_GROUP].astype(F32)
        bm = cv_ref[rows[c], 2 * D_GROUP:2 * D_GROUP + LANES]
        cm = cv_ref[rows[c], 2 * D_GROUP + LANES:N_CONV]
        xd = (x * e2[2 * c * CHUNK:(2 * c + 1) * CHUNK, :]).astype(BF16)
        cbs.append([lax.dot_general(cm[:, g * SSD_STATE:(g + 1) * SSD_STATE], bm[:, g * SSD_STATE:(g + 1) * SSD_STATE],
                                    _NT, preferred_element_type=F32) for g in range(SSD_GROUPS)])
        st_all.append([lax.dot_general(bm[:, g * SSD_STATE:(g + 1) * SSD_STATE], xd[:, g * 256:(g + 1) * 256],
                                       _TN, preferred_element_type=F32) for g in range(SSD_GROUPS)])

    ret_intra = [_retention_values(act_ref, rows[c], ret_sc[c]) for c in range(nck)]
    y_ssd = []
    scans = []
    for c in range(nck):
        xb = cv_ref[rows[c], D_GROUP:2 * D_GROUP]
        x = xb.astype(F32)
        cum = cum_all[:, lanes[c]]
        rcum = rcum_all[:, lanes[c]]
        ldiag = jnp.log(dt_t[16 * c:16 * c + SSD_HEADS, :] + dt_t[16 * c + SSD_HEADS:16 * (c + 1), :])
        ys = []
        for g in range(SSD_GROUPS):
            for pp in range(SSD_HPG // 2):
                ws = []
                for hd in (g * SSD_HPG + 2 * pp, g * SSD_HPG + 2 * pp + 1):
                    hb = SSD_HEADS + hd
                    rf = 16 * c + hd
                    rb = 16 * c + hb
                    z = jnp.where(lower, cum[:, hd:hd + 1] - sub_f[rf:rf + 1, :],
                                  jnp.where(diag, ldiag[hd:hd + 1, :], rcum[:, hb:hb + 1] - sub_b[rb:rb + 1, :]))
                    ws.append((cbs[c][g] * jnp.exp(z)).astype(BF16))
                p0 = (g * SSD_HPG + 2 * pp) * SSD_HEAD_DIM
                xp = xb[:, p0:p0 + LANES]
                zero = jnp.zeros_like(xp)
                xdiag = jnp.concatenate([jnp.where(first_head, xp, zero), jnp.where(first_head, zero, xp)], axis=0)
                ys.append(_dot(jnp.concatenate(ws, axis=1), xdiag))
        y_ssd.append(jnp.concatenate(ys, axis=1) + dsk_ref[...] * x)
        a, u = _lru_coeffs(xc[rows[c], :], r_pre[rows[c], :], i_pre[rows[c], :], bg_ref, sp)
        scans.append(_seg_scan(a, u, reverse=False))

    h, hc_ref[...] = _lru_link(scans, hc_ref[...], reverse=False)
    part_ref[:, 0:D_GROUP] = h.astype(BF16)

    for c in range(nck):
        cm = cv_ref[rows[c], 2 * D_GROUP + LANES:N_CONV]
        ecum_x = e2[(2 * c + 1) * CHUNK:(2 * c + 2) * CHUNK, :]
        cdec_x = cdec[SUBLANES * c:SUBLANES * c + 1, :]
        yo = [_dot(cm[:, g * SSD_STATE:(g + 1) * SSD_STATE], sf_ref[g].astype(BF16)) for g in range(SSD_GROUPS)]
        inter = [_dot(act_ref[rows[c], Q_OFF + hd * RET_HEAD_DIM:Q_OFF + (hd + 1) * RET_HEAD_DIM],
                      rf_ref[hd].astype(BF16)) for hd in range(RET_HEADS)]
        for g in range(SSD_GROUPS):
            sf_ref[g] = sf_ref[g] * cdec_x[:, g * 256:(g + 1) * 256] + st_all[c][g]
        for hd in range(RET_HEADS):
            rf_ref[hd] = rf_ref[hd] * math.exp(_ret_gamma_log(hd) * CHUNK) + ret_kv[c][hd]
        part_ref[rows[c], D_GROUP:2 * D_GROUP] = (y_ssd[c] + jnp.concatenate(yo, axis=1) * ecum_x).astype(BF16)
        part_ref[rows[c], 2 * D_GROUP:3 * D_GROUP] = (
            jnp.concatenate(ret_intra[c], axis=1)
            + jnp.concatenate([inter[hd] * rc_ref[RET_HEADS + hd] for hd in range(RET_HEADS)], axis=1)).astype(BF16)


def _mix_fwd(cv, qkv, dtr, wg, bg, lam, dtb, alog, dsk):
    b, s, _ = cv.shape
    tb = min(MIX_TOKENS, s)
    const = lambda bi, i: (0, 0)
    row = lambda bi, i: (bi, i, 0)
    return pl.pallas_call(
        _mixf_body,
        out_shape=jax.ShapeDtypeStruct((b, s, N_PART), BF16),
        grid=(b, s // tb),
        in_specs=[
            pl.BlockSpec((None, tb, N_CONV), row),
            pl.BlockSpec((None, tb, N_QKV), row),
            pl.BlockSpec((None, tb, LANES), row),
            pl.BlockSpec((D_GROUP // LANES, LANES, 2 * LANES), lambda bi, i: (0, 0, 0)),
            pl.BlockSpec((1, 2 * D_GROUP), const),
            pl.BlockSpec((1, D_GROUP), const),
            pl.BlockSpec((1, LANES), const),
            pl.BlockSpec((1, LANES), const),
            pl.BlockSpec((1, D_GROUP), const),
        ],
        out_specs=pl.BlockSpec((None, tb, N_PART), row),
        scratch_shapes=[
            pltpu.VMEM((tb, LANES), F32),
            pltpu.VMEM((1, D_GROUP), F32),
            pltpu.VMEM((SSD_GROUPS, SSD_STATE, SSD_HPG * SSD_HEAD_DIM), F32),
            pltpu.VMEM((RET_HEADS, RET_HEAD_DIM, RET_HEAD_DIM), F32),
            pltpu.VMEM((5 * RET_HEADS, CHUNK, CHUNK), F32),
        ],
        compiler_params=pltpu.CompilerParams(
            dimension_semantics=("arbitrary", "arbitrary"), vmem_limit_bytes=VMEM_LIMIT),
        name="mix_fwd",
    )(cv, qkv, dtr, wg, bg, lam, dtb, alog, dsk)


def _mixb_body(x_ref, act_ref, cv_ref, gate_ref, dtr_ref, part_ref,
               wg_ref, bg_ref, lam_ref, dtb_ref, alog_ref,
               snw_ref, rnw_ref, wo_ref, o_ref,
               dts_ref, hc_ref, sb_ref, rb_ref, rc_ref):
    i = pl.program_id(1)
    tb = act_ref.shape[0]
    tr, tc = _time_consts()
    nck = tb // CHUNK
    rows = [slice(c * CHUNK, (c + 1) * CHUNK) for c in range(nck)]
    lanes = rows

    @pl.when(i == 0)
    def _():
        hc_ref[...] = jnp.zeros(hc_ref.shape, F32)
        sb_ref[...] = jnp.zeros(sb_ref.shape, F32)
        rb_ref[...] = jnp.zeros(rb_ref.shape, F32)
        _ret_consts(rc_ref, tr, tc)

    xc = cv_ref[:, 0:D_GROUP].astype(F32)
    r_pre, i_pre = _lru_gate_matmul(xc, wg_ref)
    kv_all = []
    for c in range(nck):
        kvs = []
        for hd in range(RET_HEADS):
            k = act_ref[rows[c], K_OFF + hd * RET_HEAD_DIM:K_OFF + (hd + 1) * RET_HEAD_DIM]
            v = act_ref[rows[c], V_OFF + hd * RET_HEAD_DIM:V_OFF + (hd + 1) * RET_HEAD_DIM]
            kd = (k.astype(F32) * rc_ref[4 * RET_HEADS + hd]).astype(BF16)
            kvs.append(lax.dot_general(kd, v, _TN, preferred_element_type=F32))
        kv_all.append(kvs)
    dts_ref[...] = _softplus(dtr_ref[...] + dtb_ref[...])
    ltri = jnp.where(tr >= tc, 1.0, 0.0)
    ltri3 = jnp.concatenate([ltri, ltri, ltri], axis=1).astype(BF16)
    expand = _expand_mat(SSD_HEADS)
    a_row = _neg_exp_alog(alog_ref)
    last = CHUNK - 1
    dt_all = jnp.concatenate([dts_ref[r, :] for r in rows], axis=1)
    da_all = dt_all * jnp.concatenate([a_row] * nck, axis=1)
    cum_all = _cumsum_rows(ltri3, da_all)
    tot_all = cum_all[last:last + 1, :]
    rcum_all = tot_all - cum_all + da_all
    e_rows = []
    for l in lanes:
        e_rows += [jnp.exp(tot_all[:, l] - rcum_all[:, l]) * dt_all[:, l], jnp.exp(rcum_all[:, l])]
    e2 = _dot(jnp.concatenate(e_rows, axis=0).astype(BF16), expand.astype(BF16))
    cdec = _dot_hi(jnp.concatenate([jnp.broadcast_to(jnp.exp(tot_all[:, l]), (SUBLANES, LANES)) for l in lanes],
                                   axis=0), expand)
    st_all = []
    for c in range(nck):
        xb = cv_ref[rows[c], D_GROUP:2 * D_GROUP]
        bm = cv_ref[rows[c], 2 * D_GROUP:2 * D_GROUP + LANES]
        xd = (xb.astype(F32) * e2[2 * c * CHUNK:(2 * c + 1) * CHUNK, :]).astype(BF16)
        st_all.append([lax.dot_general(bm[:, g * SSD_STATE:(g + 1) * SSD_STATE], xd[:, g * 256:(g + 1) * 256],
                                       _TN, preferred_element_type=F32) for g in range(SSD_GROUPS)])

    sp = _softplus(-lam_ref[...])
    scans = [None] * nck
    y_ret = [None] * nck
    y_ssd = [None] * nck
    for c in range(nck - 1, -1, -1):
        cm = cv_ref[rows[c], 2 * D_GROUP + LANES:N_CONV]
        yo = [_dot(cm[:, g * SSD_STATE:(g + 1) * SSD_STATE], sb_ref[g].astype(BF16)) for g in range(SSD_GROUPS)]
        inter = [_dot(act_ref[rows[c], Q_OFF + hd * RET_HEAD_DIM:Q_OFF + (hd + 1) * RET_HEAD_DIM],
                      rb_ref[hd].astype(BF16)) for hd in range(RET_HEADS)]
        cdec_x = cdec[SUBLANES * c:SUBLANES * c + 1, :]
        for g in range(SSD_GROUPS):
            sb_ref[g] = sb_ref[g] * cdec_x[:, g * 256:(g + 1) * 256] + st_all[c][g]
        for hd in range(RET_HEADS):
            rb_ref[hd] = rb_ref[hd] * math.exp(_ret_gamma_log(hd) * CHUNK) + kv_all[c][hd]

        a, u = _lru_coeffs(xc[rows[c], :], r_pre[rows[c], :], i_pre[rows[c], :], bg_ref, sp)
        scans[c] = _seg_scan(a, u, reverse=True)

        ecum_x = e2[(2 * c + 1) * CHUNK:(2 * c + 2) * CHUNK, :]
        ys = part_ref[rows[c], D_GROUP:2 * D_GROUP].astype(F32) + jnp.concatenate(yo, axis=1) * ecum_x
        ys = ys * _silu(gate_ref[rows[c], D_GROUP:2 * D_GROUP].astype(F32))
        ys = ys * lax.rsqrt(jnp.mean(ys * ys, axis=-1, keepdims=True) + EPS) * snw_ref[...]
        y_ssd[c] = ys.astype(BF16)
        yr = []
        for hd in range(RET_HEADS):
            cs = slice(2 * D_GROUP + hd * RET_HEAD_DIM, 2 * D_GROUP + (hd + 1) * RET_HEAD_DIM)
            y = part_ref[rows[c], cs].astype(F32) + inter[hd] * rc_ref[3 * RET_HEADS + hd]
            mu = jnp.mean(y, axis=-1, keepdims=True)
            yc = y - mu
            var = jnp.mean(yc * yc, axis=-1, keepdims=True)
            ws = slice(hd * RET_HEAD_DIM, (hd + 1) * RET_HEAD_DIM)
            yr.append(yc * lax.rsqrt(var + EPS) * rnw_ref[:, ws] * _silu(gate_ref[rows[c], cs].astype(F32)))
        y_ret[c] = jnp.concatenate(yr, axis=1).astype(BF16)

    acc = _dot(jnp.concatenate(y_ret, axis=0), wo_ref[2 * D_GROUP:3 * D_GROUP, :])
    acc = acc + _dot(jnp.concatenate(y_ssd, axis=0), wo_ref[D_GROUP:2 * D_GROUP, :])
    h, hc_ref[...] = _lru_link(scans, hc_ref[...], reverse=True)
    y_lru = (part_ref[:, 0:D_GROUP].astype(F32) + h) * _gelu_tanh(gate_ref[:, 0:D_GROUP].astype(F32))
    o_ref[...] = x_ref[...] + (acc + _dot(y_lru.astype(BF16), wo_ref[0:D_GROUP, :]))


def _mix_bwd(x, qkv, cv, gate, dtr, part, wg, bg, lam, dtb, alog, snw, rnw, wo):
    b, s, _ = qkv.shape
    tb = min(MIX_TOKENS, s)
    nb = s // tb
    const = lambda bi, i: (0, 0)
    row = lambda bi, i: (bi, nb - 1 - i, 0)
    return pl.pallas_call(
        _mixb_body,
        out_shape=jax.ShapeDtypeStruct((b, s, D_MODEL), F32),
        grid=(b, nb),
        in_specs=[
            pl.BlockSpec((None, tb, D_MODEL), row),
            pl.BlockSpec((None, tb, N_QKV), row),
            pl.BlockSpec((None, tb, N_CONV), row),
            pl.BlockSpec((None, tb, N_GATE), row),
            pl.BlockSpec((None, tb, LANES), row),
            pl.BlockSpec((None, tb, N_PART), row),
            pl.BlockSpec((D_GROUP // LANES, LANES, 2 * LANES), lambda bi, i: (0, 0, 0)),
            pl.BlockSpec((1, 2 * D_GROUP), const),
            pl.BlockSpec((1, D_GROUP), const),
            pl.BlockSpec((1, LANES), const),
            pl.BlockSpec((1, LANES), const),
            pl.BlockSpec((1, D_GROUP), const),
            pl.BlockSpec((1, D_GROUP), const),
            pl.BlockSpec((N_PART, D_MODEL), const),
        ],
        out_specs=pl.BlockSpec((None, tb, D_MODEL), row),
        scratch_shapes=[
            pltpu.VMEM((tb, LANES), F32),
            pltpu.VMEM((1, D_GROUP), F32),
            pltpu.VMEM((SSD_GROUPS, SSD_STATE, SSD_HPG * SSD_HEAD_DIM), F32),
            pltpu.VMEM((RET_HEADS, RET_HEAD_DIM, RET_HEAD_DIM), F32),
            pltpu.VMEM((5 * RET_HEADS, CHUNK, CHUNK), F32),
        ],
        compiler_params=pltpu.CompilerParams(
            dimension_semantics=("arbitrary", "arbitrary"), vmem_limit_bytes=VMEM_LIMIT),
        name="mix_bwd",
    )(x, qkv, cv, gate, dtr, part, wg, bg, lam, dtb, alog, snw, rnw, wo)


def _to_chunk_order(x):
    b, s, d = x.shape
    return x.reshape(b, s // CHUNK, SUBLANES, SEG, d).swapaxes(2, 3).reshape(b, s, d)


def _from_chunk_order(x):
    b, s, d = x.shape
    return x.reshape(b, s // CHUNK, SEG, SUBLANES, d).swapaxes(2, 3).reshape(b, s, d)


def _block_diag(w):
    nb, bw, _ = w.shape
    eye = jnp.eye(nb, dtype=w.dtype)
    return (eye[:, None, :, None] * w[:, :, None, :]).reshape(nb * bw, nb * bw)


def _pad_lanes(v):
    return jnp.pad(v, (0, LANES - v.shape[0]))[None, :]


def _prep_layer(l, p):
    w_in = p['w_in'][l]
    o = [0, 512, 1024, 1536, 2304, 2320, 2832, 3344, 3856, 4368]
    lru_x, lru_gate, ssd_z, xbc, dtc, q, k, v, g = [w_in[:, o[j]:o[j + 1]] for j in range(9)]
    gate_w = []
    gate_b = []
    for d in range(2):
        wa = _block_diag(p['lru_w_a'][l, d])
        wi = _block_diag(p['lru_w_i'][l, d])
        gate_w.append(jnp.stack([jnp.concatenate([w[j:j + LANES, j:j + LANES] for w in (wa, wi)], axis=1)
                                 for j in range(0, D_GROUP, LANES)]))
        gate_b.append(jnp.concatenate([p['lru_b_a'][l, d], p['lru_b_i'][l, d]])[None, :])
    return dict(
        ffn1_nw=p['ffn1_norm'][l][None, :], ffn1_wgu=p['ffn1_w_gu'][l].astype(BF16), ffn1_wd=p['ffn1_w_down'][l].astype(BF16),
        ffn2_nw=p['ffn2_norm'][l][None, :], ffn2_wgu=p['ffn2_w_gu'][l].astype(BF16), ffn2_wd=p['ffn2_w_down'][l].astype(BF16),
        mix_nw=p['mix_norm'][l][None, :],
        wc=jnp.concatenate([lru_x, xbc], axis=1).astype(BF16),
        wq=jnp.concatenate([q, k, v], axis=1).astype(BF16),
        wg=jnp.concatenate([lru_gate, ssd_z, g], axis=1).astype(BF16),
        wdt=jnp.pad(dtc, ((0, 0), (0, LANES - dtc.shape[1]))).astype(BF16),
        cw=jnp.concatenate([p['lru_conv_w'][l], p['ssd_conv_w'][l]], axis=1),
        cb=jnp.concatenate([p['lru_conv_b'][l], p['ssd_conv_b'][l]])[None, :],
        gate_w=[w.astype(BF16) for w in gate_w], gate_b=gate_b,
        lam=[p['lru_lam'][l, d][None, :] for d in range(2)],
        dtb=_pad_lanes(p['ssd_dt_bias'][l].reshape(-1)),
        alog=_pad_lanes(p['ssd_a_log'][l].reshape(-1)),
        dsk=jnp.repeat(p['ssd_d'][l], SSD_HEAD_DIM)[None, :],
        snw=p['ssd_norm'][l][None, :], rnw=p['ret_norm'][l][None, :],
        wo=p['w_out'][l].astype(BF16),
    )


def _rope_tables(s):
    d = RET_HEAD_DIM
    inv_freq = 1.0 / (ROPE_BASE ** (jnp.arange(0, d, 2, dtype=F32) / d))
    r = jnp.arange(s, dtype=jnp.int32)
    local = r % CHUNK
    pos = (r // CHUNK) * CHUNK + (local % SUBLANES) * SEG + local // SUBLANES
    ang = pos.astype(F32)[:, None] * inv_freq[None, :]
    cos = jnp.cos(ang)
    sin = jnp.sin(ang)
    return jnp.concatenate([cos, cos], axis=1), jnp.concatenate([-sin, sin], axis=1)


def _trunk(x, layers, final_nw):
    b, s, d = x.shape
    c2, s2 = _rope_tables(s)
    x = _to_chunk_order(x)
    for li, lp in enumerate(layers):
        x = _ffn(x.reshape(b * s, d), lp['ffn1_nw'], lp['ffn1_wgu'], lp['ffn1_wd']).reshape(b, s, d)
        cv, qkv, gate, dtr = _proj(x, lp['mix_nw'], lp['wc'], lp['wq'], lp['wg'], lp['wdt'], c2, s2, lp['cw'], lp['cb'])
        part = _mix_fwd(cv, qkv, dtr, lp['gate_w'][0], lp['gate_b'][0], lp['lam'][0], lp['dtb'], lp['alog'], lp['dsk'])
        x = _mix_bwd(x, qkv, cv, gate, dtr, part, lp['gate_w'][1], lp['gate_b'][1], lp['lam'][1],
                     lp['dtb'], lp['alog'], lp['snw'], lp['rnw'], lp['wo'])
        x = _ffn(x.reshape(b * s, d), lp['ffn2_nw'], lp['ffn2_wgu'], lp['ffn2_wd'],
                 final_nw if li == len(layers) - 1 else None).reshape(b, s, d)
    return _from_chunk_order(x)


def kernel(x_prompt, x_sample, ffn1_norm, ffn1_w_gu, ffn1_w_down, mix_norm, w_in, lru_conv_w, lru_conv_b, lru_w_a, lru_b_a, lru_w_i, lru_b_i, lru_lam, ssd_conv_w, ssd_conv_b, ssd_dt_bias, ssd_a_log, ssd_d, ssd_norm, ret_norm, w_out, ffn2_norm, ffn2_w_gu, ffn2_w_down, final_norm):
    p = dict(ffn1_norm=ffn1_norm, ffn1_w_gu=ffn1_w_gu, ffn1_w_down=ffn1_w_down, mix_norm=mix_norm, w_in=w_in,
             lru_conv_w=lru_conv_w, lru_conv_b=lru_conv_b, lru_w_a=lru_w_a, lru_b_a=lru_b_a, lru_w_i=lru_w_i,
             lru_b_i=lru_b_i, lru_lam=lru_lam, ssd_conv_w=ssd_conv_w, ssd_conv_b=ssd_conv_b, ssd_dt_bias=ssd_dt_bias,
             ssd_a_log=ssd_a_log, ssd_d=ssd_d, ssd_norm=ssd_norm, ret_norm=ret_norm, w_out=w_out,
             ffn2_norm=ffn2_norm, ffn2_w_gu=ffn2_w_gu, ffn2_w_down=ffn2_w_down)
    layers = [_prep_layer(l, p) for l in range(ffn1_norm.shape[0])]
    final_nw = final_norm[None, :]
    return (_trunk(x_prompt, layers, final_nw), _trunk(x_sample, layers, final_nw))
```

```python
import math

import jax
import jax.numpy as jnp
from jax import lax
from jax.experimental import pallas as pl
from jax.experimental.pallas import tpu as pltpu

F32 = jnp.float32
BF16 = jnp.bfloat16

D_MODEL = 1024
D_FF = 2816
D_GROUP = 512
EPS = 1e-6
CONV_W = 4
LRU_C = 8.0
SSD_HEADS = 8
SSD_HEAD_DIM = 64
SSD_STATE = 64
SSD_GROUPS = 2
SSD_HPG = SSD_HEADS // SSD_GROUPS
SSD_XBC = D_GROUP + 2 * SSD_GROUPS * SSD_STATE
RET_HEADS = 4
RET_HEAD_DIM = 128
CHUNK = 128
ROPE_BASE = 10000.0

LANES = 128
SUBLANES = 8
BF16_SUBLANES = 16
SEG = CHUNK // SUBLANES
HALO = BF16_SUBLANES
VMEM_LIMIT = 56 * 1024 * 1024

N_CONV = D_GROUP + SSD_XBC
N_QKV = 3 * D_GROUP
N_GATE = 3 * D_GROUP
Q_OFF = 0
K_OFF = D_GROUP
V_OFF = 2 * D_GROUP
N_PART = 3 * D_GROUP

FFN_TOKENS = 1024
PROJ_TOKENS = 1024
MIX_TOKENS = 1024
FF_CHUNK = 512
TINY = 1e-30

_NT = (((1,), (1,)), ((), ()))
_TN = (((0,), (0,)), ((), ()))


def _dot(a, b):
    return jnp.dot(a, b, preferred_element_type=F32)


def _dot_hi(a, b):
    return jnp.dot(a, b, preferred_element_type=F32, precision=lax.Precision.HIGHEST)


def _rms(x, w):
    return x * lax.rsqrt(jnp.mean(x * x, axis=-1, keepdims=True) + EPS) * w


def _sigmoid(x):
    return 0.5 * jnp.tanh(0.5 * x) + 0.5


def _silu(x):
    h = 0.5 * x
    return h * jnp.tanh(h) + h


def _cumsum_rows(tri3, x):
    hi = x.astype(BF16)
    r1 = x - hi.astype(F32)
    mid = r1.astype(BF16)
    lo = (r1 - mid.astype(F32)).astype(BF16)
    return _dot(tri3, jnp.concatenate([hi, mid, lo], axis=0))


def _softplus(x):
    return jnp.maximum(x, 0.0) + jnp.log(1.0 + jnp.exp(-jnp.abs(x)))


def _gelu_tanh(x):
    c = math.sqrt(2.0 / math.pi)
    return 0.5 * x * (1.0 + jnp.tanh(c * (x + 0.044715 * (x * x * x))))


def _ffn_body(x_ref, nw_ref, wgu_ref, wd_ref, *rest):
    o_ref = rest[-1]
    x = x_ref[...]
    xb = _rms(x, nw_ref[...]).astype(BF16)
    acc = jnp.zeros(x.shape, F32)
    for c in range(0, D_FF, FF_CHUNK):
        w = min(FF_CHUNK, D_FF - c)
        g = _dot(xb, wgu_ref[:, c:c + w])
        u = _dot(xb, wgu_ref[:, D_FF + c:D_FF + c + w])
        a = (_silu(g) * u).astype(BF16)
        acc = acc + _dot(a, wd_ref[c:c + w, :])
    y = x + 0.5 * acc
    o_ref[...] = _rms(y, rest[0][...]) if len(rest) == 2 else y


def _ffn_weight_specs():
    const = lambda *_: (0, 0)
    return [
        pl.BlockSpec((1, D_MODEL), const),
        pl.BlockSpec((D_MODEL, 2 * D_FF), const, pipeline_mode=pl.Buffered(1)),
        pl.BlockSpec((D_FF, D_MODEL), const, pipeline_mode=pl.Buffered(1)),
    ]


def _ffn(x, nw, wgu, wd, final_nw=None):
    t = x.shape[0]
    tm = min(FFN_TOKENS, t)
    extra = [] if final_nw is None else [final_nw]
    return pl.pallas_call(
        _ffn_body,
        out_shape=jax.ShapeDtypeStruct(x.shape, F32),
        grid=(t // tm,),
        in_specs=[pl.BlockSpec((tm, D_MODEL), lambda i: (i, 0))] + _ffn_weight_specs()
        + [pl.BlockSpec((1, D_MODEL), lambda i: (0, 0))] * len(extra),
        out_specs=pl.BlockSpec((tm, D_MODEL), lambda i: (i, 0)),
        compiler_params=pltpu.CompilerParams(
            dimension_semantics=("arbitrary",), vmem_limit_bytes=VMEM_LIMIT),
        name="ffn",
    )(x, nw, wgu, wd, *extra)


def _proj_body(x_ref, xp_ref, xn_ref, nw_ref, wc_ref, wq_ref, wg_ref, wdt_ref, c2_ref, s2_ref, cw_ref, cb_ref,
               cv_ref, qkv_ref, g_ref, dt_ref):
    i = pl.program_id(1)
    n = pl.num_programs(1)
    tm = x_ref.shape[0]
    nw = nw_ref[...]
    xb = _rms(x_ref[...], nw).astype(BF16)
    xp = _rms(xp_ref[...] * jnp.where(i == 0, 0.0, 1.0), nw).astype(BF16)
    xn = _rms(xn_ref[...] * jnp.where(i == n - 1, 0.0, 1.0), nw).astype(BF16)
    pc = _dot(jnp.concatenate([xp, xb, xn], axis=0), wc_ref[...])
    pq = _dot(xb, wq_ref[...])
    g_ref[...] = _dot(xb, wg_ref[...]).astype(BF16)
    dt_ref[...] = _dot(xb, wdt_ref[...])
    raw = pc[HALO:HALO + tm, :]
    n1, p1, p2 = _conv_edges(raw, pc[0:HALO, :], pc[HALO + tm:HALO + tm + HALO, :])
    for c in range(tm // CHUNK):
        rows = slice(c * CHUNK, (c + 1) * CHUNK)
        seg = slice(c * SUBLANES, (c + 1) * SUBLANES)
        conv = _conv_chunk(raw[rows, :], n1[seg, :], p1[seg, :], p2[seg, :], cw_ref, cb_ref)
        cv_ref[rows, 0:D_GROUP] = conv[:, 0:D_GROUP].astype(BF16)
        cv_ref[rows, D_GROUP:N_CONV] = _silu(conv[:, D_GROUP:N_CONV]).astype(BF16)
    qkv_ref[:, V_OFF:N_QKV] = pq[:, V_OFF:N_QKV].astype(BF16)
    c2 = c2_ref[...]
    s2 = s2_ref[...]
    kscale = RET_HEAD_DIM ** -0.5
    for h in range(2 * RET_HEADS):
        lo = h * RET_HEAD_DIM
        blk = pq[:, lo:lo + RET_HEAD_DIM]
        rot = blk * c2 + pltpu.roll(blk, RET_HEAD_DIM // 2, 1) * s2
        if h >= RET_HEADS:
            rot = rot * kscale
        qkv_ref[:, lo:lo + RET_HEAD_DIM] = rot.astype(BF16)


def _proj(x, nw, wc, wq, wg, wdt, c2, s2, cw, cb):
    b, s, _ = x.shape
    tm = min(PROJ_TOKENS, s)
    per = tm // HALO
    last = s // HALO - 1
    const = lambda bi, i: (0, 0)
    row = lambda bi, i: (bi, i, 0)
    return pl.pallas_call(
        _proj_body,
        out_shape=(jax.ShapeDtypeStruct((b, s, N_CONV), BF16),
                   jax.ShapeDtypeStruct((b, s, N_QKV), BF16),
                   jax.ShapeDtypeStruct((b, s, N_GATE), BF16),
                   jax.ShapeDtypeStruct((b, s, LANES), F32)),
        grid=(b, s // tm),
        in_specs=[
            pl.BlockSpec((None, tm, D_MODEL), row),
            pl.BlockSpec((None, HALO, D_MODEL), lambda bi, i: (bi, jnp.maximum(i * per - 1, 0), 0)),
            pl.BlockSpec((None, HALO, D_MODEL), lambda bi, i: (bi, jnp.minimum((i + 1) * per, last), 0)),
            pl.BlockSpec((1, D_MODEL), const),
            pl.BlockSpec((D_MODEL, N_CONV), const, pipeline_mode=pl.Buffered(1)),
            pl.BlockSpec((D_MODEL, N_QKV), const, pipeline_mode=pl.Buffered(1)),
            pl.BlockSpec((D_MODEL, N_GATE), const, pipeline_mode=pl.Buffered(1)),
            pl.BlockSpec((D_MODEL, LANES), const, pipeline_mode=pl.Buffered(1)),
            pl.BlockSpec((tm, LANES), lambda bi, i: (i, 0)),
            pl.BlockSpec((tm, LANES), lambda bi, i: (i, 0)),
            pl.BlockSpec((CONV_W, N_CONV), const),
            pl.BlockSpec((1, N_CONV), const),
        ],
        out_specs=(pl.BlockSpec((None, tm, N_CONV), row),
                   pl.BlockSpec((None, tm, N_QKV), row),
                   pl.BlockSpec((None, tm, N_GATE), row),
                   pl.BlockSpec((None, tm, LANES), row)),
        compiler_params=pltpu.CompilerParams(
            dimension_semantics=("arbitrary", "arbitrary"), vmem_limit_bytes=VMEM_LIMIT),
        name="mix_proj",
    )(x, x, x, nw, wc, wq, wg, wdt, c2, s2, cw, cb)


def _split(x):
    return x.reshape(x.shape[0] // CHUNK, SEG, SUBLANES, x.shape[1])


def _seg_rows(x4, j):
    return x4[:, j].reshape(-1, x4.shape[3])


def _conv_edges(x, prev, nxt):
    x4 = _split(x)
    nck, _, _, ch = x4.shape
    nseg = nck * SUBLANES
    rows = lax.broadcasted_iota(jnp.int32, (nseg, ch), 0)
    n1 = jnp.where(rows == nseg - 1, nxt[0:1, :], pltpu.roll(_seg_rows(x4, 0), nseg - 1, 0))
    p1 = jnp.where(rows == 0, prev[HALO - 1:HALO, :], pltpu.roll(_seg_rows(x4, SEG - 1), 1, 0))
    p2 = jnp.where(rows == 0, prev[HALO - 1 - SUBLANES:HALO - SUBLANES, :],
                   pltpu.roll(_seg_rows(x4, SEG - 2), 1, 0))
    return n1, p1, p2


def _conv_chunk(x, n1, p1, p2, cw_ref, cb_ref):
    x3 = x.reshape(SEG, SUBLANES, x.shape[1])
    xp1 = jnp.concatenate([x3[1:SEG], n1[None]], axis=0)
    xm1 = jnp.concatenate([p1[None], x3[0:SEG - 1]], axis=0)
    xm2 = jnp.concatenate([p2[None], p1[None], x3[0:SEG - 2]], axis=0)
    w = cw_ref[...]
    out = cb_ref[...] + xm2 * w[0:1, :] + xm1 * w[1:2, :] + x3 * w[2:3, :] + xp1 * w[3:4, :]
    return out.reshape(x.shape)


def _lru_gate_matmul(xc, wg_ref):
    xb = xc.astype(BF16)
    rs = []
    gs = []
    for p in range(D_GROUP // LANES):
        g = _dot(xb[:, p * LANES:(p + 1) * LANES], wg_ref[p])
        rs.append(g[:, 0:LANES])
        gs.append(g[:, LANES:2 * LANES])
    return jnp.concatenate(rs, axis=1), jnp.concatenate(gs, axis=1)


def _lru_coeffs(xc, r_pre, i_pre, bg_ref, sp):
    r = _sigmoid(r_pre + bg_ref[:, 0:D_GROUP])
    ig = _sigmoid(i_pre + bg_ref[:, D_GROUP:2 * D_GROUP])
    a = jnp.exp((-LRU_C) * r * sp)
    om = 1.0 - a * a
    u = om * lax.rsqrt(jnp.maximum(om, TINY)) * (ig * xc)
    return a, u


def _row_scan(a, u, reverse):
    n = a.shape[0]
    rows = lax.broadcasted_iota(jnp.int32, a.shape, 0)
    s = 1
    while s < n:
        if reverse:
            a_s = pltpu.roll(a, n - s, 0)
            u_s = pltpu.roll(u, n - s, 0)
            m = rows < n - s
        else:
            a_s = pltpu.roll(a, s, 0)
            u_s = pltpu.roll(u, s, 0)
            m = rows >= s
        u = jnp.where(m, a * u_s + u, u)
        a = jnp.where(m, a * a_s, a)
        s *= 2
    return a, u


def _seg_scan(a, u, reverse):
    a3 = a.reshape(SEG, SUBLANES, a.shape[1])
    u3 = u.reshape(SEG, SUBLANES, u.shape[1])
    acc_a = [None] * SEG
    acc_u = [None] * SEG
    pa = pu = None
    for j in (range(SEG - 1, -1, -1) if reverse else range(SEG)):
        aj = a3[j]
        uj = u3[j]
        if pa is not None:
            uj = aj * pu + uj
            aj = aj * pa
        acc_a[j], acc_u[j] = aj, uj
        pa, pu = aj, uj
    return acc_a, acc_u


def _lru_link(scans, carry, reverse):
    end = 0 if reverse else SEG - 1
    tot_a = jnp.concatenate([sc[0][end] for sc in scans], axis=0)
    tot_u = jnp.concatenate([sc[1][end] for sc in scans], axis=0)
    nseg = tot_a.shape[0]
    pa, hend = _row_scan(tot_a, tot_u, reverse)
    hend = hend + pa * carry
    rows = lax.broadcasted_iota(jnp.int32, hend.shape, 0)
    if reverse:
        cin = jnp.where(rows == nseg - 1, carry, pltpu.roll(hend, nseg - 1, 0))
        new_carry = hend[0:1, :]
    else:
        cin = jnp.where(rows == 0, carry, pltpu.roll(hend, 1, 0))
        new_carry = hend[nseg - 1:nseg, :]
    hs = []
    for c, (acc_a, acc_u) in enumerate(scans):
        cc = cin[c * SUBLANES:(c + 1) * SUBLANES, :]
        hs += [acc_u[j] + acc_a[j] * cc for j in range(SEG)]
    return jnp.concatenate(hs, axis=0), new_carry


def _time_consts():
    r = lax.broadcasted_iota(jnp.int32, (CHUNK, CHUNK), 0)
    c = lax.broadcasted_iota(jnp.int32, (CHUNK, CHUNK), 1)
    tr = (r % SUBLANES) * SEG + r // SUBLANES
    tc = (c % SUBLANES) * SEG + c // SUBLANES
    return tr, tc


def _expand_mat(first_row):
    r = lax.broadcasted_iota(jnp.int32, (LANES, D_GROUP), 0)
    c = lax.broadcasted_iota(jnp.int32, (LANES, D_GROUP), 1)
    return jnp.where(r == first_row + c // SSD_HEAD_DIM, 1.0, 0.0)


def _ret_gamma_log(h):
    return math.log1p(-(2.0 ** (-5.0 - h)))


def _neg_exp_alog(alog_ref):
    lane = lax.broadcasted_iota(jnp.int32, (1, LANES), 1)
    return jnp.where(lane < 2 * SSD_HEADS, -jnp.exp(alog_ref[...]), 0.0)


def _ret_consts(rc_ref, tr, tc):
    dist = jnp.abs(tr - tc).astype(F32)
    pos = tr.astype(F32)
    for hd in range(RET_HEADS):
        lg = _ret_gamma_log(hd)
        rc_ref[hd] = jnp.exp(lg * dist)
        rc_ref[RET_HEADS + hd] = jnp.exp(lg * (pos + 1.0))
        rc_ref[2 * RET_HEADS + hd] = jnp.exp(lg * (CHUNK - 1.0 - pos))
        rc_ref[3 * RET_HEADS + hd] = jnp.exp(lg * (CHUNK - pos))
        rc_ref[4 * RET_HEADS + hd] = jnp.exp(lg * pos)


def _retention_scores(act_ref, rc_ref, rows):
    zeros_h = jnp.zeros((CHUNK, RET_HEAD_DIM), BF16)
    scs = []
    kvs = []
    for hp in range(RET_HEADS // 2):
        lo = 2 * hp * RET_HEAD_DIM
        q2 = act_ref[rows, Q_OFF + lo:Q_OFF + lo + 2 * RET_HEAD_DIM]
        k2 = act_ref[rows, K_OFF + lo:K_OFF + lo + 2 * RET_HEAD_DIM]
        v2 = act_ref[rows, V_OFF + lo:V_OFF + lo + 2 * RET_HEAD_DIM]
        ka, kb = k2[:, 0:RET_HEAD_DIM], k2[:, RET_HEAD_DIM:]
        kdiag = jnp.concatenate([jnp.concatenate([ka, zeros_h], axis=1),
                                 jnp.concatenate([zeros_h, kb], axis=1)], axis=0)
        sc = lax.dot_general(q2, kdiag, _NT, preferred_element_type=F32)
        dec = jnp.concatenate([rc_ref[2 * hp], rc_ref[2 * hp + 1]], axis=1)
        scs.append((sc * dec).astype(BF16))
        for j, kk in enumerate((ka, kb)):
            kd = (kk.astype(F32) * rc_ref[2 * RET_HEADS + 2 * hp + j]).astype(BF16)
            kvs.append(lax.dot_general(kd, v2[:, j * RET_HEAD_DIM:(j + 1) * RET_HEAD_DIM], _TN,
                                       preferred_element_type=F32))
    return scs, kvs


def _retention_values(act_ref, rows, scs):
    zeros_h = jnp.zeros((CHUNK, RET_HEAD_DIM), BF16)
    ys = []
    for hp in range(RET_HEADS // 2):
        lo = 2 * hp * RET_HEAD_DIM
        v2 = act_ref[rows, V_OFF + lo:V_OFF + lo + 2 * RET_HEAD_DIM]
        vdiag = jnp.concatenate([jnp.concatenate([v2[:, 0:RET_HEAD_DIM], zeros_h], axis=1),
                                 jnp.concatenate([zeros_h, v2[:, RET_HEAD_DIM:]], axis=1)], axis=0)
        ys.append(_dot(scs[hp], vdiag))
    return ys


def _mixf_body(cv_ref, act_ref, dtr_ref, wg_ref, bg_ref, lam_ref,
               dtb_ref, alog_ref, dsk_ref, part_ref,
               dts_ref, hc_ref, sf_ref, rf_ref, rc_ref):
    i = pl.program_id(1)
    tb = act_ref.shape[0]
    tr, tc = _time_consts()

    @pl.when(i == 0)
    def _():
        hc_ref[...] = jnp.zeros(hc_ref.shape, F32)
        sf_ref[...] = jnp.zeros(sf_ref.shape, F32)
        rf_ref[...] = jnp.zeros(rf_ref.shape, F32)
        _ret_consts(rc_ref, tr, tc)

    nck = tb // CHUNK
    rows = [slice(c * CHUNK, (c + 1) * CHUNK) for c in range(nck)]
    lanes = rows

    ret_sc = []
    ret_kv = []
    for c in range(nck):
        scs, kvs = _retention_scores(act_ref, rc_ref, rows[c])
        ret_sc.append(scs)
        ret_kv.append(kvs)

    xc = cv_ref[:, 0:D_GROUP].astype(F32)
    r_pre, i_pre = _lru_gate_matmul(xc, wg_ref)
    sp = _softplus(-lam_ref[...])
    dts_ref[...] = _softplus(dtr_ref[...] + dtb_ref[...])

    lower = tr > tc
    diag = tr == tc
    ltri = jnp.where(tr >= tc, 1.0, 0.0)
    ltri3 = jnp.concatenate([ltri, ltri, ltri], axis=1).astype(BF16)
    utri = jnp.where(tr <= tc, 1.0, 0.0)
    expand = _expand_mat(0)
    a_row = _neg_exp_alog(alog_ref)
    last = CHUNK - 1
    lane = lax.broadcasted_iota(jnp.int32, (CHUNK, LANES), 1)
    first_head = lane < SSD_HEAD_DIM

    dt_all = jnp.concatenate([dts_ref[r, :] for r in rows], axis=1)
    da_all = dt_all * jnp.concatenate([a_row] * nck, axis=1)
    cum_all = _cumsum_rows(ltri3, da_all)
    tot_all = cum_all[last:last + 1, :]
    rcum_all = tot_all - cum_all + da_all
    da_t = jnp.concatenate([da_all[:, l].T[0:BF16_SUBLANES, :] for l in lanes], axis=0)
    dt_t = jnp.concatenate([dt_all[:, l].T[0:BF16_SUBLANES, :] for l in lanes], axis=0)
    cum_t = _dot_hi(da_t, utri)
    rcum_t = cum_t[:, last:last + 1] - cum_t + da_t
    ldt = jnp.log(dt_t)
    sub_f = cum_t - ldt
    sub_b = rcum_t - ldt
    e_rows = []
    for l in lanes:
        e_rows += [jnp.exp(tot_all[:, l] - cum_all[:, l]) * dt_all[:, l], jnp.exp(cum_all[:, l])]
    e2 = _dot(jnp.concatenate(e_rows, axis=0).astype(BF16), expand.astype(BF16))
    cdec = _dot_hi(jnp.concatenate([jnp.broadcast_to(jnp.exp(tot_all[:, l]), (SUBLANES, LANES)) for l in lanes],
                                   axis=0), expand)

    cbs = []
    st_all = []
    for c in range(nck):
        x = cv_ref[rows[c], D_GROUP:2 * D_GROUP].astype(F32)
        bm = cv_ref[rows[c], 2 * D_GROUP:2 * D_GROUP + LANES]
        cm = cv_ref[rows[c], 2 * D_GROUP + LANES:N_CONV]
        xd = (x * e2[2 * c * CHUNK:(2 * c + 1) * CHUNK, :]).astype(BF16)
        cbs.append([lax.dot_general(cm[:, g * SSD_STATE:(g + 1) * SSD_STATE], bm[:, g * SSD_STATE:(g + 1) * SSD_STATE],
                                    _NT, preferred_element_type=F32) for g in range(SSD_GROUPS)])
        st_all.append([lax.dot_general(bm[:, g * SSD_STATE:(g + 1) * SSD_STATE], xd[:, g * 256:(g + 1) * 256],
                                       _TN, preferred_element_type=F32) for g in range(SSD_GROUPS)])

    ret_intra = [_retention_values(act_ref, rows[c], ret_sc[c]) for c in range(nck)]
    y_ssd = []
    scans = []
    for c in range(nck):
        xb = cv_ref[rows[c], D_GROUP:2 * D_GROUP]
        x = xb.astype(F32)
        cum = cum_all[:, lanes[c]]
        rcum = rcum_all[:, lanes[c]]
        ldiag = jnp.log(dt_t[16 * c:16 * c + SSD_HEADS, :] + dt_t[16 * c + SSD_HEADS:16 * (c + 1), :])
        ys = []
        for g in range(SSD_GROUPS):
            for pp in range(SSD_HPG // 2):
                ws = []
                for hd in (g * SSD_HPG + 2 * pp, g * SSD_HPG + 2 * pp + 1):
                    hb = SSD_HEADS + hd
                    rf = 16 * c + hd
                    rb = 16 * c + hb
                    z = jnp.where(lower, cum[:, hd:hd + 1] - sub_f[rf:rf + 1, :],
                                  jnp.where(diag, ldiag[hd:hd + 1, :], rcum[:, hb:hb + 1] - sub_b[rb:rb + 1, :]))
                    ws.append((cbs[c][g] * jnp.exp(z)).astype(BF16))
                p0 = (g * SSD_HPG + 2 * pp) * SSD_HEAD_DIM
                xp = xb[:, p0:p0 + LANES]
                zero = jnp.zeros_like(xp)
                xdiag = jnp.concatenate([jnp.where(first_head, xp, zero), jnp.where(first_head, zero, xp)], axis=0)
                ys.append(_dot(jnp.concatenate(ws, axis=1), xdiag))
        y_ssd.append(jnp.concatenate(ys, axis=1) + dsk_ref[...] * x)
        a, u = _lru_coeffs(xc[rows[c], :], r_pre[rows[c], :], i_pre[rows[c], :], bg_ref, sp)
        scans.append(_seg_scan(a, u, reverse=False))

    h, hc_ref[...] = _lru_link(scans, hc_ref[...], reverse=False)
    part_ref[:, 0:D_GROUP] = h.astype(BF16)

    for c in range(nck):
        cm = cv_ref[rows[c], 2 * D_GROUP + LANES:N_CONV]
        ecum_x = e2[(2 * c + 1) * CHUNK:(2 * c + 2) * CHUNK, :]
        cdec_x = cdec[SUBLANES * c:SUBLANES * c + 1, :]
        yo = [_dot(cm[:, g * SSD_STATE:(g + 1) * SSD_STATE], sf_ref[g].astype(BF16)) for g in range(SSD_GROUPS)]
        inter = [_dot(act_ref[rows[c], Q_OFF + hd * RET_HEAD_DIM:Q_OFF + (hd + 1) * RET_HEAD_DIM],
                      rf_ref[hd].astype(BF16)) for hd in range(RET_HEADS)]
        for g in range(SSD_GROUPS):
            sf_ref[g] = sf_ref[g] * cdec_x[:, g * 256:(g + 1) * 256] + st_all[c][g]
        for hd in range(RET_HEADS):
            rf_ref[hd] = rf_ref[hd] * math.exp(_ret_gamma_log(hd) * CHUNK) + ret_kv[c][hd]
        part_ref[rows[c], D_GROUP:2 * D_GROUP] = (y_ssd[c] + jnp.concatenate(yo, axis=1) * ecum_x).astype(BF16)
        part_ref[rows[c], 2 * D_GROUP:3 * D_GROUP] = (
            jnp.concatenate(ret_intra[c], axis=1)
            + jnp.concatenate([inter[hd] * rc_ref[RET_HEADS + hd] for hd in range(RET_HEADS)], axis=1)).astype(BF16)


def _mix_fwd(cv, qkv, dtr, wg, bg, lam, dtb, alog, dsk):
    b, s, _ = cv.shape
    tb = min(MIX_TOKENS, s)
    const = lambda bi, i: (0, 0)
    row = lambda bi, i: (bi, i, 0)
    return pl.pallas_call(
        _mixf_body,
        out_shape=jax.ShapeDtypeStruct((b, s, N_PART), BF16),
        grid=(b, s // tb),
        in_specs=[
            pl.BlockSpec((None, tb, N_CONV), row),
            pl.BlockSpec((None, tb, N_QKV), row),
            pl.BlockSpec((None, tb, LANES), row),
            pl.BlockSpec((D_GROUP // LANES, LANES, 2 * LANES), lambda bi, i: (0, 0, 0)),
            pl.BlockSpec((1, 2 * D_GROUP), const),
            pl.BlockSpec((1, D_GROUP), const),
            pl.BlockSpec((1, LANES), const),
            pl.BlockSpec((1, LANES), const),
            pl.BlockSpec((1, D_GROUP), const),
        ],
        out_specs=pl.BlockSpec((None, tb, N_PART), row),
        scratch_shapes=[
            pltpu.VMEM((tb, LANES), F32),
            pltpu.VMEM((1, D_GROUP), F32),
            pltpu.VMEM((SSD_GROUPS, SSD_STATE, SSD_HPG * SSD_HEAD_DIM), F32),
            pltpu.VMEM((RET_HEADS, RET_HEAD_DIM, RET_HEAD_DIM), F32),
            pltpu.VMEM((5 * RET_HEADS, CHUNK, CHUNK), F32),
        ],
        compiler_params=pltpu.CompilerParams(
            dimension_semantics=("arbitrary", "arbitrary"), vmem_limit_bytes=VMEM_LIMIT),
        name="mix_fwd",
    )(cv, qkv, dtr, wg, bg, lam, dtb, alog, dsk)


def _mixb_body(x_ref, act_ref, cv_ref, gate_ref, dtr_ref, part_ref,
               wg_ref, bg_ref, lam_ref, dtb_ref, alog_ref,
               snw_ref, rnw_ref, wo_ref, o_ref,
               dts_ref, hc_ref, sb_ref, rb_ref, rc_ref):
    i = pl.program_id(1)
    tb = act_ref.shape[0]
    tr, tc = _time_consts()
    nck = tb // CHUNK
    rows = [slice(c * CHUNK, (c + 1) * CHUNK) for c in range(nck)]
    lanes = rows

    @pl.when(i == 0)
    def _():
        hc_ref[...] = jnp.zeros(hc_ref.shape, F32)
        sb_ref[...] = jnp.zeros(sb_ref.shape, F32)
        rb_ref[...] = jnp.zeros(rb_ref.shape, F32)
        _ret_consts(rc_ref, tr, tc)

    xc = cv_ref[:, 0:D_GROUP].astype(F32)
    r_pre, i_pre = _lru_gate_matmul(xc, wg_ref)
    kv_all = []
    for c in range(nck):
        kvs = []
        for hd in range(RET_HEADS):
            k = act_ref[rows[c], K_OFF + hd * RET_HEAD_DIM:K_OFF + (hd + 1) * RET_HEAD_DIM]
            v = act_ref[rows[c], V_OFF + hd * RET_HEAD_DIM:V_OFF + (hd + 1) * RET_HEAD_DIM]
            kd = (k.astype(F32) * rc_ref[4 * RET_HEADS + hd]).astype(BF16)
            kvs.append(lax.dot_general(kd, v, _TN, preferred_element_type=F32))
        kv_all.append(kvs)
    dts_ref[...] = _softplus(dtr_ref[...] + dtb_ref[...])
    ltri = jnp.where(tr >= tc, 1.0, 0.0)
    ltri3 = jnp.concatenate([ltri, ltri, ltri], axis=1).astype(BF16)
    expand = _expand_mat(SSD_HEADS)
    a_row = _neg_exp_alog(alog_ref)
    last = CHUNK - 1
    dt_all = jnp.concatenate([dts_ref[r, :] for r in rows], axis=1)
    da_all = dt_all * jnp.concatenate([a_row] * nck, axis=1)
    cum_all = _cumsum_rows(ltri3, da_all)
    tot_all = cum_all[last:last + 1, :]
    rcum_all = tot_all - cum_all + da_all
    e_rows = []
    for l in lanes:
        e_rows += [jnp.exp(tot_all[:, l] - rcum_all[:, l]) * dt_all[:, l], jnp.exp(rcum_all[:, l])]
    e2 = _dot(jnp.concatenate(e_rows, axis=0).astype(BF16), expand.astype(BF16))
    cdec = _dot_hi(jnp.concatenate([jnp.broadcast_to(jnp.exp(tot_all[:, l]), (SUBLANES, LANES)) for l in lanes],
                                   axis=0), expand)
    st_all = []
    for c in range(nck):
        xb = cv_ref[rows[c], D_GROUP:2 * D_GROUP]
        bm = cv_ref[rows[c], 2 * D_GROUP:2 * D_GROUP + LANES]
        xd = (xb.astype(F32) * e2[2 * c * CHUNK:(2 * c + 1) * CHUNK, :]).astype(BF16)
        st_all.append([lax.dot_general(bm[:, g * SSD_STATE:(g + 1) * SSD_STATE], xd[:, g * 256:(g + 1) * 256],
                                       _TN, preferred_element_type=F32) for g in range(SSD_GROUPS)])

    sp = _softplus(-lam_ref[...])
    scans = [None] * nck
    y_ret = [None] * nck
    y_ssd = [None] * nck
    for c in range(nck - 1, -1, -1):
        cm = cv_ref[rows[c], 2 * D_GROUP + LANES:N_CONV]
        yo = [_dot(cm[:, g * SSD_STATE:(g + 1) * SSD_STATE], sb_ref[g].astype(BF16)) for g in range(SSD_GROUPS)]
        inter = [_dot(act_ref[rows[c], Q_OFF + hd * RET_HEAD_DIM:Q_OFF + (hd + 1) * RET_HEAD_DIM],
                      rb_ref[hd].astype(BF16)) for hd in range(RET_HEADS)]
        cdec_x = cdec[SUBLANES * c:SUBLANES * c + 1, :]
        for g in range(SSD_GROUPS):
            sb_ref[g] = sb_ref[g] * cdec_x[:, g * 256:(g + 1) * 256] + st_all[c][g]
        for hd in range(RET_HEADS):
            rb_ref[hd] = rb_ref[hd] * math.exp(_ret_gamma_log(hd) * CHUNK) + kv_all[c][hd]

        a, u = _lru_coeffs(xc[rows[c], :], r_pre[rows[c], :], i_pre[rows[c], :], bg_ref, sp)
        scans[c] = _seg_scan(a, u, reverse=True)

        ecum_x = e2[(2 * c + 1) * CHUNK:(2 * c + 2) * CHUNK, :]
        ys = part_ref[rows[c], D_GROUP:2 * D_GROUP].astype(F32) + jnp.concatenate(yo, axis=1) * ecum_x
        ys = ys * _silu(gate_ref[rows[c], D_GROUP:2 * D_GROUP].astype(F32))
        ys = ys * lax.rsqrt(jnp.mean(ys * ys, axis=-1, keepdims=True) + EPS) * snw_ref[...]
        y_ssd[c] = ys.astype(BF16)
        yr = []
        for hd in range(RET_HEADS):
            cs = slice(2 * D_GROUP + hd * RET_HEAD_DIM, 2 * D_GROUP + (hd + 1) * RET_HEAD_DIM)
            y = part_ref[rows[c], cs].astype(F32) + inter[hd] * rc_ref[3 * RET_HEADS + hd]
            mu = jnp.mean(y, axis=-1, keepdims=True)
            yc = y - mu
            var = jnp.mean(yc * yc, axis=-1, keepdims=True)
            ws = slice(hd * RET_HEAD_DIM, (hd + 1) * RET_HEAD_DIM)
            yr.append(yc * lax.rsqrt(var + EPS) * rnw_ref[:, ws] * _silu(gate_ref[rows[c], cs].astype(F32)))
        y_ret[c] = jnp.concatenate(yr, axis=1).astype(BF16)

    acc = _dot(jnp.concatenate(y_ret, axis=0), wo_ref[2 * D_GROUP:3 * D_GROUP, :])
    acc = acc + _dot(jnp.concatenate(y_ssd, axis=0), wo_ref[D_GROUP:2 * D_GROUP, :])
    h, hc_ref[...] = _lru_link(scans, hc_ref[...], reverse=True)
    y_lru = (part_ref[:, 0:D_GROUP].astype(F32) + h) * _gelu_tanh(gate_ref[:, 0:D_GROUP].astype(F32))
    o_ref[...] = x_ref[...] + (acc + _dot(y_lru.astype(BF16), wo_ref[0:D_GROUP, :]))


def _mix_bwd(x, qkv, cv, gate, dtr, part, wg, bg, lam, dtb, alog, snw, rnw, wo):
    b, s, _ = qkv.shape
    tb = min(MIX_TOKENS, s)
    nb = s // tb
    const = lambda bi, i: (0, 0)
    row = lambda bi, i: (bi, nb - 1 - i, 0)
    return pl.pallas_call(
        _mixb_body,
        out_shape=jax.ShapeDtypeStruct((b, s, D_MODEL), F32),
        grid=(b, nb),
        in_specs=[
            pl.BlockSpec((None, tb, D_MODEL), row),
            pl.BlockSpec((None, tb, N_QKV), row),
            pl.BlockSpec((None, tb, N_CONV), row),
            pl.BlockSpec((None, tb, N_GATE), row),
            pl.BlockSpec((None, tb, LANES), row),
            pl.BlockSpec((None, tb, N_PART), row),
            pl.BlockSpec((D_GROUP // LANES, LANES, 2 * LANES), lambda bi, i: (0, 0, 0)),
            pl.BlockSpec((1, 2 * D_GROUP), const),
            pl.BlockSpec((1, D_GROUP), const),
            pl.BlockSpec((1, LANES), const),
            pl.BlockSpec((1, LANES), const),
            pl.BlockSpec((1, D_GROUP), const),
            pl.BlockSpec((1, D_GROUP), const),
            pl.BlockSpec((N_PART, D_MODEL), const),
        ],
        out_specs=pl.BlockSpec((None, tb, D_MODEL), row),
        scratch_shapes=[
            pltpu.VMEM((tb, LANES), F32),
            pltpu.VMEM((1, D_GROUP), F32),
            pltpu.VMEM((SSD_GROUPS, SSD_STATE, SSD_HPG * SSD_HEAD_DIM), F32),
            pltpu.VMEM((RET_HEADS, RET_HEAD_DIM, RET_HEAD_DIM), F32),
            pltpu.VMEM((5 * RET_HEADS, CHUNK, CHUNK), F32),
        ],
        compiler_params=pltpu.CompilerParams(
            dimension_semantics=("arbitrary", "arbitrary"), vmem_limit_bytes=VMEM_LIMIT),
        name="mix_bwd",
    )(x, qkv, cv, gate, dtr, part, wg, bg, lam, dtb, alog, snw, rnw, wo)


def _to_chunk_order(x):
    b, s, d = x.shape
    return x.reshape(b, s // CHUNK, SUBLANES, SEG, d).swapaxes(2, 3).reshape(b, s, d)


def _from_chunk_order(x):
    b, s, d = x.shape
    return x.reshape(b, s // CHUNK, SEG, SUBLANES, d).swapaxes(2, 3).reshape(b, s, d)


def _block_diag(w):
    nb, bw, _ = w.shape
    eye = jnp.eye(nb, dtype=w.dtype)
    return (eye[:, None, :, None] * w[:, :, None, :]).reshape(nb * bw, nb * bw)


def _pad_lanes(v):
    return jnp.pad(v, (0, LANES - v.shape[0]))[None, :]


def _prep_layer(l, p):
    w_in = p['w_in'][l]
    o = [0, 512, 1024, 1536, 2304, 2320, 2832, 3344, 3856, 4368]
    lru_x, lru_gate, ssd_z, xbc, dtc, q, k, v, g = [w_in[:, o[j]:o[j + 1]] for j in range(9)]
    gate_w = []
    gate_b = []
    for d in range(2):
        wa = _block_diag(p['lru_w_a'][l, d])
        wi = _block_diag(p['lru_w_i'][l, d])
        gate_w.append(jnp.stack([jnp.concatenate([w[j:j + LANES, j:j + LANES] for w in (wa, wi)], axis=1)
                                 for j in range(0, D_GROUP, LANES)]))
        gate_b.append(jnp.concatenate([p['lru_b_a'][l, d], p['lru_b_i'][l, d]])[None, :])
    return dict(
        ffn1_nw=p['ffn1_norm'][l][None, :], ffn1_wgu=p['ffn1_w_gu'][l].astype(BF16), ffn1_wd=p['ffn1_w_down'][l].astype(BF16),
        ffn2_nw=p['ffn2_norm'][l][None, :], ffn2_wgu=p['ffn2_w_gu'][l].astype(BF16), ffn2_wd=p['ffn2_w_down'][l].astype(BF16),
        mix_nw=p['mix_norm'][l][None, :],
        wc=jnp.concatenate([lru_x, xbc], axis=1).astype(BF16),
        wq=jnp.concatenate([q, k, v], axis=1).astype(BF16),
        wg=jnp.concatenate([lru_gate, ssd_z, g], axis=1).astype(BF16),
        wdt=jnp.pad(dtc, ((0, 0), (0, LANES - dtc.shape[1]))).astype(BF16),
        cw=jnp.concatenate([p['lru_conv_w'][l], p['ssd_conv_w'][l]], axis=1),
        cb=jnp.concatenate([p['lru_conv_b'][l], p['ssd_conv_b'][l]])[None, :],
        gate_w=[w.astype(BF16) for w in gate_w], gate_b=gate_b,
        lam=[p['lru_lam'][l, d][None, :] for d in range(2)],
        dtb=_pad_lanes(p['ssd_dt_bias'][l].reshape(-1)),
        alog=_pad_lanes(p['ssd_a_log'][l].reshape(-1)),
        dsk=jnp.repeat(p['ssd_d'][l], SSD_HEAD_DIM)[None, :],
        snw=p['ssd_norm'][l][None, :], rnw=p['ret_norm'][l][None, :],
        wo=p['w_out'][l].astype(BF16),
    )


def _rope_tables(s):
    d = RET_HEAD_DIM
    inv_freq = 1.0 / (ROPE_BASE ** (jnp.arange(0, d, 2, dtype=F32) / d))
    r = jnp.arange(s, dtype=jnp.int32)
    local = r % CHUNK
    pos = (r // CHUNK) * CHUNK + (local % SUBLANES) * SEG + local // SUBLANES
    ang = pos.astype(F32)[:, None] * inv_freq[None, :]
    cos = jnp.cos(ang)
    sin = jnp.sin(ang)
    return jnp.concatenate([cos, cos], axis=1), jnp.concatenate([-sin, sin], axis=1)


def _trunk(x, layers, final_nw):
    b, s, d = x.shape
    c2, s2 = _rope_tables(s)
    x = _to_chunk_order(x)
    for li, lp in enumerate(layers):
        x = _ffn(x.reshape(b * s, d), lp['ffn1_nw'], lp['ffn1_wgu'], lp['ffn1_wd']).reshape(b, s, d)
        cv, qkv, gate, dtr = _proj(x, lp['mix_nw'], lp['wc'], lp['wq'], lp['wg'], lp['wdt'], c2, s2, lp['cw'], lp['cb'])
        part = _mix_fwd(cv, qkv, dtr, lp['gate_w'][0], lp['gate_b'][0], lp['lam'][0], lp['dtb'], lp['alog'], lp['dsk'])
        x = _mix_bwd(x, qkv, cv, gate, dtr, part, lp['gate_w'][1], lp['gate_b'][1], lp['lam'][1],
                     lp['dtb'], lp['alog'], lp['snw'], lp['rnw'], lp['wo'])
        x = _ffn(x.reshape(b * s, d), lp['ffn2_nw'], lp['ffn2_wgu'], lp['ffn2_wd'],
                 final_nw if li == len(layers) - 1 else None).reshape(b, s, d)
    return _from_chunk_order(x)


def kernel(x_prompt, x_sample, ffn1_norm, ffn1_w_gu, ffn1_w_down, mix_norm, w_in, lru_conv_w, lru_conv_b, lru_w_a, lru_b_a, lru_w_i, lru_b_i, lru_lam, ssd_conv_w, ssd_conv_b, ssd_dt_bias, ssd_a_log, ssd_d, ssd_norm, ret_norm, w_out, ffn2_norm, ffn2_w_gu, ffn2_w_down, final_norm):
    p = dict(ffn1_norm=ffn1_norm, ffn1_w_gu=ffn1_w_gu, ffn1_w_down=ffn1_w_down, mix_norm=mix_norm, w_in=w_in,
             lru_conv_w=lru_conv_w, lru_conv_b=lru_conv_b, lru_w_a=lru_w_a, lru_b_a=lru_b_a, lru_w_i=lru_w_i,
             lru_b_i=lru_b_i, lru_lam=lru_lam, ssd_conv_w=ssd_conv_w, ssd_conv_b=ssd_conv_b, ssd_dt_bias=ssd_dt_bias,
             ssd_a_log=ssd_a_log, ssd_d=ssd_d, ssd_norm=ssd_norm, ret_norm=ret_norm, w_out=w_out,
             ffn2_norm=ffn2_norm, ffn2_w_gu=ffn2_w_gu, ffn2_w_down=ffn2_w_down)
    layers = [_prep_layer(l, p) for l in range(ffn1_norm.shape[0])]
    final_nw = final_norm[None, :]
    return (_trunk(x_prompt, layers, final_nw), _trunk(x_sample, layers, final_nw))
```

```python
import math

import jax
import jax.numpy as jnp
from jax import lax
from jax.experimental import pallas as pl
from jax.experimental.pallas import tpu as pltpu

F32 = jnp.float32
BF16 = jnp.bfloat16

D_MODEL = 1024
D_FF = 2816
D_GROUP = 512
EPS = 1e-6
CONV_W = 4
LRU_C = 8.0
SSD_HEADS = 8
SSD_HEAD_DIM = 64
SSD_STATE = 64
SSD_GROUPS = 2
SSD_HPG = SSD_HEADS // SSD_GROUPS
SSD_XBC = D_GROUP + 2 * SSD_GROUPS * SSD_STATE
RET_HEADS = 4
RET_HEAD_DIM = 128
CHUNK = 128
ROPE_BASE = 10000.0

LANES = 128
SUBLANES = 8
BF16_SUBLANES = 16
SEG = CHUNK // SUBLANES
HALO = BF16_SUBLANES
VMEM_LIMIT = 56 * 1024 * 1024

N_CONV = D_GROUP + SSD_XBC
N_QKV = 3 * D_GROUP
N_GATE = 3 * D_GROUP
Q_OFF = 0
K_OFF = D_GROUP
V_OFF = 2 * D_GROUP
N_PART = 3 * D_GROUP

FFN_TOKENS = 1024
PROJ_TOKENS = 1024
MIX_TOKENS = 1024
FF_CHUNK = 512
TINY = 1e-30
LOG2E = math.log2(math.e)

_NT = (((1,), (1,)), ((), ()))
_TN = (((0,), (0,)), ((), ()))


def _dot(a, b):
    return jnp.dot(a, b, preferred_element_type=F32)


def _dot_hi(a, b):
    return jnp.dot(a, b, preferred_element_type=F32, precision=lax.Precision.HIGHEST)


def _rms(x, w):
    return x * lax.rsqrt(jnp.mean(x * x, axis=-1, keepdims=True) + EPS) * w


def _sigmoid(x):
    return 0.5 * jnp.tanh(0.5 * x) + 0.5


def _silu(x):
    h = 0.5 * x
    return h * jnp.tanh(h) + h


def _cumsum_rows(tri3, x):
    hi = x.astype(BF16)
    r1 = x - hi.astype(F32)
    mid = r1.astype(BF16)
    lo = (r1 - mid.astype(F32)).astype(BF16)
    return _dot(tri3, jnp.concatenate([hi, mid, lo], axis=0))


def _softplus(x):
    return jnp.maximum(x, 0.0) + jnp.log(1.0 + jnp.exp(-jnp.abs(x)))


def _gelu_tanh(x):
    c = math.sqrt(2.0 / math.pi)
    return 0.5 * x * (1.0 + jnp.tanh(c * (x + 0.044715 * (x * x * x))))


def _layer_spec(shape, l, **kw):
    zeros = (0,) * len(shape)
    return pl.BlockSpec((None,) + tuple(shape), lambda *_: (l,) + zeros, **kw)


def _ffn_body(x_ref, nw_ref, wgu_ref, wd_ref, *rest):
    o_ref = rest[-1]
    x = x_ref[...]
    xb = _rms(x, nw_ref[...]).astype(BF16)
    acc = jnp.zeros(x.shape, F32)
    for c in range(0, D_FF, FF_CHUNK):
        w = min(FF_CHUNK, D_FF - c)
        g = _dot(xb, wgu_ref[:, c:c + w])
        u = _dot(xb, wgu_ref[:, D_FF + c:D_FF + c + w])
        a = (_silu(g) * u).astype(BF16)
        acc = acc + _dot(a, wd_ref[c:c + w, :])
    y = x + 0.5 * acc
    o_ref[...] = _rms(y, rest[0][...]) if len(rest) == 2 else y


def _ffn(x, l, nw, wgu, wd, final_nw=None):
    b, s, _ = x.shape
    tm = min(FFN_TOKENS, s)
    row = lambda bi, i: (bi, i, 0)
    extra = [] if final_nw is None else [final_nw]
    return pl.pallas_call(
        _ffn_body,
        out_shape=jax.ShapeDtypeStruct(x.shape, F32),
        grid=(b, s // tm),
        in_specs=[pl.BlockSpec((None, tm, D_MODEL), row),
                  _layer_spec((1, D_MODEL), l),
                  _layer_spec((D_MODEL, 2 * D_FF), l, pipeline_mode=pl.Buffered(1)),
                  _layer_spec((D_FF, D_MODEL), l, pipeline_mode=pl.Buffered(1))]
        + [pl.BlockSpec((1, D_MODEL), lambda bi, i: (0, 0))] * len(extra),
        out_specs=pl.BlockSpec((None, tm, D_MODEL), row),
        compiler_params=pltpu.CompilerParams(
            dimension_semantics=("arbitrary", "arbitrary"), vmem_limit_bytes=VMEM_LIMIT),
        name="ffn",
    )(x, nw, wgu, wd, *extra)


def _proj_body(x_ref, xp_ref, xn_ref, nw_ref, wc_ref, wq_ref, wg_ref, wdt_ref, c2_ref, s2_ref, cw_ref, cb_ref,
               cv_ref, qkv_ref, g_ref, dt_ref):
    i = pl.program_id(1)
    n = pl.num_programs(1)
    tm = x_ref.shape[0]
    nw = nw_ref[...]
    xb = _rms(x_ref[...], nw).astype(BF16)
    xp = _rms(xp_ref[...] * jnp.where(i == 0, 0.0, 1.0), nw).astype(BF16)
    xn = _rms(xn_ref[...] * jnp.where(i == n - 1, 0.0, 1.0), nw).astype(BF16)
    pc = _dot(jnp.concatenate([xp, xb, xn], axis=0), wc_ref[...])
    pq = _dot(xb, wq_ref[...])
    g_ref[...] = _dot(xb, wg_ref[...]).astype(BF16)
    dt_ref[...] = _dot(xb, wdt_ref[...])
    raw = pc[HALO:HALO + tm, :]
    n1, p1, p2 = _conv_edges(raw, pc[0:HALO, :], pc[HALO + tm:HALO + tm + HALO, :])
    for c in range(tm // CHUNK):
        rows = slice(c * CHUNK, (c + 1) * CHUNK)
        seg = slice(c * SUBLANES, (c + 1) * SUBLANES)
        conv = _conv_chunk(raw[rows, :], n1[seg, :], p1[seg, :], p2[seg, :], cw_ref, cb_ref)
        cv_ref[rows, 0:D_GROUP] = conv[:, 0:D_GROUP].astype(BF16)
        cv_ref[rows, D_GROUP:N_CONV] = _silu(conv[:, D_GROUP:N_CONV]).astype(BF16)
    qkv_ref[:, V_OFF:N_QKV] = pq[:, V_OFF:N_QKV].astype(BF16)
    c2 = c2_ref[...]
    s2 = s2_ref[...]
    kscale = RET_HEAD_DIM ** -0.5
    for h in range(2 * RET_HEADS):
        lo = h * RET_HEAD_DIM
        blk = pq[:, lo:lo + RET_HEAD_DIM]
        rot = blk * c2 + pltpu.roll(blk, RET_HEAD_DIM // 2, 1) * s2
        if h >= RET_HEADS:
            rot = rot * kscale
        qkv_ref[:, lo:lo + RET_HEAD_DIM] = rot.astype(BF16)


def _proj(x, l, nw, wc, wq, wg, wdt, c2, s2, cw, cb):
    b, s, _ = x.shape
    tm = min(PROJ_TOKENS, s)
    per = tm // HALO
    last = s // HALO - 1
    row = lambda bi, i: (bi, i, 0)
    return pl.pallas_call(
        _proj_body,
        out_shape=(jax.ShapeDtypeStruct((b, s, N_CONV), BF16),
                   jax.ShapeDtypeStruct((b, s, N_QKV), BF16),
                   jax.ShapeDtypeStruct((b, s, N_GATE), BF16),
                   jax.ShapeDtypeStruct((b, s, LANES), F32)),
        grid=(b, s // tm),
        in_specs=[
            pl.BlockSpec((None, tm, D_MODEL), row),
            pl.BlockSpec((None, HALO, D_MODEL), lambda bi, i: (bi, jnp.maximum(i * per - 1, 0), 0)),
            pl.BlockSpec((None, HALO, D_MODEL), lambda bi, i: (bi, jnp.minimum((i + 1) * per, last), 0)),
            _layer_spec((1, D_MODEL), l),
            _layer_spec((D_MODEL, N_CONV), l, pipeline_mode=pl.Buffered(1)),
            _layer_spec((D_MODEL, N_QKV), l, pipeline_mode=pl.Buffered(1)),
            _layer_spec((D_MODEL, N_GATE), l, pipeline_mode=pl.Buffered(1)),
            _layer_spec((D_MODEL, LANES), l, pipeline_mode=pl.Buffered(1)),
            pl.BlockSpec((tm, LANES), lambda bi, i: (i, 0)),
            pl.BlockSpec((tm, LANES), lambda bi, i: (i, 0)),
            _layer_spec((CONV_W, N_CONV), l),
            _layer_spec((1, N_CONV), l),
        ],
        out_specs=(pl.BlockSpec((None, tm, N_CONV), row),
                   pl.BlockSpec((None, tm, N_QKV), row),
                   pl.BlockSpec((None, tm, N_GATE), row),
                   pl.BlockSpec((None, tm, LANES), row)),
        compiler_params=pltpu.CompilerParams(
            dimension_semantics=("arbitrary", "arbitrary"), vmem_limit_bytes=VMEM_LIMIT),
        name="mix_proj",
    )(x, x, x, nw, wc, wq, wg, wdt, c2, s2, cw, cb)


def _split(x):
    return x.reshape(x.shape[0] // CHUNK, SEG, SUBLANES, x.shape[1])


def _seg_rows(x4, j):
    return x4[:, j].reshape(-1, x4.shape[3])


def _conv_edges(x, prev, nxt):
    x4 = _split(x)
    nck, _, _, ch = x4.shape
    nseg = nck * SUBLANES
    rows = lax.broadcasted_iota(jnp.int32, (nseg, ch), 0)
    n1 = jnp.where(rows == nseg - 1, nxt[0:1, :], pltpu.roll(_seg_rows(x4, 0), nseg - 1, 0))
    p1 = jnp.where(rows == 0, prev[HALO - 1:HALO, :], pltpu.roll(_seg_rows(x4, SEG - 1), 1, 0))
    p2 = jnp.where(rows == 0, prev[HALO - 1 - SUBLANES:HALO - SUBLANES, :],
                   pltpu.roll(_seg_rows(x4, SEG - 2), 1, 0))
    return n1, p1, p2


def _conv_chunk(x, n1, p1, p2, cw_ref, cb_ref):
    x3 = x.reshape(SEG, SUBLANES, x.shape[1])
    xp1 = jnp.concatenate([x3[1:SEG], n1[None]], axis=0)
    xm1 = jnp.concatenate([p1[None], x3[0:SEG - 1]], axis=0)
    xm2 = jnp.concatenate([p2[None], p1[None], x3[0:SEG - 2]], axis=0)
    w = cw_ref[...]
    out = cb_ref[...] + xm2 * w[0:1, :] + xm1 * w[1:2, :] + x3 * w[2:3, :] + xp1 * w[3:4, :]
    return out.reshape(x.shape)


def _lru_gate_matmul(xc, wg_ref):
    xb = xc.astype(BF16)
    rs = []
    gs = []
    for p in range(D_GROUP // LANES):
        g = _dot(xb[:, p * LANES:(p + 1) * LANES], wg_ref[p])
        rs.append(g[:, 0:LANES])
        gs.append(g[:, LANES:2 * LANES])
    return jnp.concatenate(rs, axis=1), jnp.concatenate(gs, axis=1)


def _lru_rate(lam_ref):
    return _softplus(-lam_ref[...]) * (-LRU_C * LOG2E)


def _lru_coeffs(xc, r_pre, i_pre, bg_ref, rate):
    r = _sigmoid(r_pre + bg_ref[:, 0:D_GROUP])
    ig = _sigmoid(i_pre + bg_ref[:, D_GROUP:2 * D_GROUP])
    a = jnp.exp2(r * rate)
    om = 1.0 - a * a
    u = om * lax.rsqrt(jnp.maximum(om, TINY)) * (ig * xc)
    return a, u


def _row_scan(a, u, reverse):
    n = a.shape[0]
    rows = lax.broadcasted_iota(jnp.int32, a.shape, 0)
    s = 1
    while s < n:
        if reverse:
            a_s = pltpu.roll(a, n - s, 0)
            u_s = pltpu.roll(u, n - s, 0)
            m = rows < n - s
        else:
            a_s = pltpu.roll(a, s, 0)
            u_s = pltpu.roll(u, s, 0)
            m = rows >= s
        u = jnp.where(m, a * u_s + u, u)
        a = jnp.where(m, a * a_s, a)
        s *= 2
    return a, u


def _seg_scan(a, u, reverse):
    a3 = a.reshape(SEG, SUBLANES, a.shape[1])
    u3 = u.reshape(SEG, SUBLANES, u.shape[1])
    acc_a = [None] * SEG
    acc_u = [None] * SEG
    pa = pu = None
    for j in (range(SEG - 1, -1, -1) if reverse else range(SEG)):
        aj = a3[j]
        uj = u3[j]
        if pa is not None:
            uj = aj * pu + uj
            aj = aj * pa
        acc_a[j], acc_u[j] = aj, uj
        pa, pu = aj, uj
    return acc_a, acc_u


def _lru_link(scans, carry, reverse):
    end = 0 if reverse else SEG - 1
    tot_a = jnp.concatenate([sc[0][end] for sc in scans], axis=0)
    tot_u = jnp.concatenate([sc[1][end] for sc in scans], axis=0)
    nseg = tot_a.shape[0]
    pa, hend = _row_scan(tot_a, tot_u, reverse)
    hend = hend + pa * carry
    rows = lax.broadcasted_iota(jnp.int32, hend.shape, 0)
    if reverse:
        cin = jnp.where(rows == nseg - 1, carry, pltpu.roll(hend, nseg - 1, 0))
        new_carry = hend[0:1, :]
    else:
        cin = jnp.where(rows == 0, carry, pltpu.roll(hend, 1, 0))
        new_carry = hend[nseg - 1:nseg, :]
    hs = []
    for c, (acc_a, acc_u) in enumerate(scans):
        cc = cin[c * SUBLANES:(c + 1) * SUBLANES, :]
        hs += [acc_u[j] + acc_a[j] * cc for j in range(SEG)]
    return jnp.concatenate(hs, axis=0), new_carry


def _time_consts():
    r = lax.broadcasted_iota(jnp.int32, (CHUNK, CHUNK), 0)
    c = lax.broadcasted_iota(jnp.int32, (CHUNK, CHUNK), 1)
    tr = (r % SUBLANES) * SEG + r // SUBLANES
    tc = (c % SUBLANES) * SEG + c // SUBLANES
    return tr, tc


def _expand_mat(first_row):
    r = lax.broadcasted_iota(jnp.int32, (LANES, D_GROUP), 0)
    c = lax.broadcasted_iota(jnp.int32, (LANES, D_GROUP), 1)
    return jnp.where(r == first_row + c // SSD_HEAD_DIM, 1.0, 0.0)


def _ret_gamma_log(h):
    return math.log1p(-(2.0 ** (-5.0 - h)))


def _neg_exp_alog(alog_ref):
    lane = lax.broadcasted_iota(jnp.int32, (1, LANES), 1)
    return jnp.where(lane < 2 * SSD_HEADS, -jnp.exp(alog_ref[...]), 0.0)


def _ret_consts(rc_ref, tr, tc):
    dist = jnp.abs(tr - tc).astype(F32)
    pos = tr.astype(F32)
    for hd in range(RET_HEADS):
        lg = _ret_gamma_log(hd)
        rc_ref[hd] = jnp.exp(lg * dist)
        rc_ref[RET_HEADS + hd] = jnp.exp(lg * (pos + 1.0))
        rc_ref[2 * RET_HEADS + hd] = jnp.exp(lg * (CHUNK - 1.0 - pos))
        rc_ref[3 * RET_HEADS + hd] = jnp.exp(lg * (CHUNK - pos))
        rc_ref[4 * RET_HEADS + hd] = jnp.exp(lg * pos)


def _retention_scores(act_ref, rc_ref, rows):
    zeros_h = jnp.zeros((CHUNK, RET_HEAD_DIM), BF16)
    scs = []
    kvs = []
    for hp in range(RET_HEADS // 2):
        lo = 2 * hp * RET_HEAD_DIM
        q2 = act_ref[rows, Q_OFF + lo:Q_OFF + lo + 2 * RET_HEAD_DIM]
        k2 = act_ref[rows, K_OFF + lo:K_OFF + lo + 2 * RET_HEAD_DIM]
        v2 = act_ref[rows, V_OFF + lo:V_OFF + lo + 2 * RET_HEAD_DIM]
        ka, kb = k2[:, 0:RET_HEAD_DIM], k2[:, RET_HEAD_DIM:]
        kdiag = jnp.concatenate([jnp.concatenate([ka, zeros_h], axis=1),
                                 jnp.concatenate([zeros_h, kb], axis=1)], axis=0)
        sc = lax.dot_general(q2, kdiag, _NT, preferred_element_type=F32)
        dec = jnp.concatenate([rc_ref[2 * hp], rc_ref[2 * hp + 1]], axis=1)
        scs.append((sc * dec).astype(BF16))
        for j, kk in enumerate((ka, kb)):
            kd = (kk.astype(F32) * rc_ref[2 * RET_HEADS + 2 * hp + j]).astype(BF16)
            kvs.append(lax.dot_general(kd, v2[:, j * RET_HEAD_DIM:(j + 1) * RET_HEAD_DIM], _TN,
                                       preferred_element_type=F32))
    return scs, kvs


def _retention_values(act_ref, rows, scs):
    zeros_h = jnp.zeros((CHUNK, RET_HEAD_DIM), BF16)
    ys = []
    for hp in range(RET_HEADS // 2):
        lo = 2 * hp * RET_HEAD_DIM
        v2 = act_ref[rows, V_OFF + lo:V_OFF + lo + 2 * RET_HEAD_DIM]
        vdiag = jnp.concatenate([jnp.concatenate([v2[:, 0:RET_HEAD_DIM], zeros_h], axis=1),
                                 jnp.concatenate([zeros_h, v2[:, RET_HEAD_DIM:]], axis=1)], axis=0)
        ys.append(_dot(scs[hp], vdiag))
    return ys


def _mixf_body(cv_ref, act_ref, dtr_ref, wg_ref, bg_ref, lam_ref,
               dtb_ref, alog_ref, dsk_ref, part_ref,
               dts_ref, hc_ref, sf_ref, rf_ref, rc_ref):
    i = pl.program_id(1)
    tb = act_ref.shape[0]
    tr, tc = _time_consts()

    @pl.when(i == 0)
    def _():
        hc_ref[...] = jnp.zeros(hc_ref.shape, F32)
        sf_ref[...] = jnp.zeros(sf_ref.shape, F32)
        rf_ref[...] = jnp.zeros(rf_ref.shape, F32)
        _ret_consts(rc_ref, tr, tc)

    nck = tb // CHUNK
    rows = [slice(c * CHUNK, (c + 1) * CHUNK) for c in range(nck)]
    lanes = rows

    ret_sc = []
    ret_kv = []
    for c in range(nck):
        scs, kvs = _retention_scores(act_ref, rc_ref, rows[c])
        ret_sc.append(scs)
        ret_kv.append(kvs)

    xc = cv_ref[:, 0:D_GROUP].astype(F32)
    r_pre, i_pre = _lru_gate_matmul(xc, wg_ref)
    sp = _lru_rate(lam_ref)
    dts_ref[...] = _softplus(dtr_ref[...] + dtb_ref[...])

    lower = tr > tc
    diag = tr == tc
    ltri = jnp.where(tr >= tc, 1.0, 0.0)
    ltri3 = jnp.concatenate([ltri, ltri, ltri], axis=1).astype(BF16)
    utri = jnp.where(tr <= tc, 1.0, 0.0)
    expand = _expand_mat(0)
    a_row = _neg_exp_alog(alog_ref)
    last = CHUNK - 1
    lane = lax.broadcasted_iota(jnp.int32, (CHUNK, LANES), 1)
    first_head = lane < SSD_HEAD_DIM

    dt_all = jnp.concatenate([dts_ref[r, :] for r in rows], axis=1)
    da_all = dt_all * jnp.concatenate([a_row * LOG2E] * nck, axis=1)
    cum_all = _cumsum_rows(ltri3, da_all)
    tot_all = cum_all[last:last + 1, :]
    rcum_all = tot_all - cum_all + da_all
    da_t = jnp.concatenate([da_all[:, l].T[0:BF16_SUBLANES, :] for l in lanes], axis=0)
    dt_t = jnp.concatenate([dt_all[:, l].T[0:BF16_SUBLANES, :] for l in lanes], axis=0)
    cum_t = _dot_hi(da_t, utri)
    rcum_t = cum_t[:, last:last + 1] - cum_t + da_t
    ldt = jnp.log2(dt_t)
    sub_f = cum_t - ldt
    sub_b = rcum_t - ldt
    e_rows = []
    for l in lanes:
        e_rows += [jnp.exp2(tot_all[:, l] - cum_all[:, l]) * dt_all[:, l], jnp.exp2(cum_all[:, l])]
    e2 = _dot(jnp.concatenate(e_rows, axis=0).astype(BF16), expand.astype(BF16))
    cdec = _dot_hi(jnp.concatenate([jnp.broadcast_to(jnp.exp2(tot_all[:, l]), (SUBLANES, LANES)) for l in lanes],
                                   axis=0), expand)

    cbs = []
    st_all = []
    for c in range(nck):
        xb = cv_ref[rows[c], D_GROUP:2 * D_GROUP]
        bm = cv_ref[rows[c], 2 * D_GROUP:2 * D_GROUP + LANES]
        cm = cv_ref[rows[c], 2 * D_GROUP + LANES:N_CONV]
        xd = xb * e2[2 * c * CHUNK:(2 * c + 1) * CHUNK, :].astype(BF16)
        cbs.append([lax.dot_general(cm[:, g * SSD_STATE:(g + 1) * SSD_STATE], bm[:, g * SSD_STATE:(g + 1) * SSD_STATE],
                                    _NT, preferred_element_type=F32) for g in range(SSD_GROUPS)])
        st_all.append([lax.dot_general(bm[:, g * SSD_STATE:(g + 1) * SSD_STATE], xd[:, g * 256:(g + 1) * 256],
                                       _TN, preferred_element_type=F32) for g in range(SSD_GROUPS)])

    ret_intra = [_retention_values(act_ref, rows[c], ret_sc[c]) for c in range(nck)]
    y_ssd = []
    scans = []
    for c in range(nck):
        xb = cv_ref[rows[c], D_GROUP:2 * D_GROUP]
        x = xb.astype(F32)
        cum = cum_all[:, lanes[c]]
        rcum = rcum_all[:, lanes[c]]
        ldiag = jnp.log2(dt_t[16 * c:16 * c + SSD_HEADS, :] + dt_t[16 * c + SSD_HEADS:16 * (c + 1), :])
        ys = []
        for g in range(SSD_GROUPS):
            for pp in range(SSD_HPG // 2):
                ws = []
                for hd in (g * SSD_HPG + 2 * pp, g * SSD_HPG + 2 * pp + 1):
                    hb = SSD_HEADS + hd
                    rf = 16 * c + hd
                    rb = 16 * c + hb
                    z = jnp.where(lower, cum[:, hd:hd + 1] - sub_f[rf:rf + 1, :],
                                  jnp.where(diag, ldiag[hd:hd + 1, :], rcum[:, hb:hb + 1] - sub_b[rb:rb + 1, :]))
                    ws.append((cbs[c][g] * jnp.exp2(z)).astype(BF16))
                p0 = (g * SSD_HPG + 2 * pp) * SSD_HEAD_DIM
                xp = xb[:, p0:p0 + LANES]
                zero = jnp.zeros_like(xp)
                xdiag = jnp.concatenate([jnp.where(first_head, xp, zero), jnp.where(first_head, zero, xp)], axis=0)
                ys.append(_dot(jnp.concatenate(ws, axis=1), xdiag))
        y_ssd.append(jnp.concatenate(ys, axis=1) + dsk_ref[...] * x)
        a, u = _lru_coeffs(xc[rows[c], :], r_pre[rows[c], :], i_pre[rows[c], :], bg_ref, sp)
        scans.append(_seg_scan(a, u, reverse=False))

    h, hc_ref[...] = _lru_link(scans, hc_ref[...], reverse=False)
    part_ref[:, 0:D_GROUP] = h.astype(BF16)

    for c in range(nck):
        cm = cv_ref[rows[c], 2 * D_GROUP + LANES:N_CONV]
        ecum_x = e2[(2 * c + 1) * CHUNK:(2 * c + 2) * CHUNK, :]
        cdec_x = cdec[SUBLANES * c:SUBLANES * c + 1, :]
        yo = [_dot(cm[:, g * SSD_STATE:(g + 1) * SSD_STATE], sf_ref[g].astype(BF16)) for g in range(SSD_GROUPS)]
        inter = [_dot(act_ref[rows[c], Q_OFF + hd * RET_HEAD_DIM:Q_OFF + (hd + 1) * RET_HEAD_DIM],
                      rf_ref[hd].astype(BF16)) for hd in range(RET_HEADS)]
        for g in range(SSD_GROUPS):
            sf_ref[g] = sf_ref[g] * cdec_x[:, g * 256:(g + 1) * 256] + st_all[c][g]
        for hd in range(RET_HEADS):
            rf_ref[hd] = rf_ref[hd] * math.exp(_ret_gamma_log(hd) * CHUNK) + ret_kv[c][hd]
        part_ref[rows[c], D_GROUP:2 * D_GROUP] = (y_ssd[c] + jnp.concatenate(yo, axis=1) * ecum_x).astype(BF16)
        part_ref[rows[c], 2 * D_GROUP:3 * D_GROUP] = (
            jnp.concatenate(ret_intra[c], axis=1)
            + jnp.concatenate([inter[hd] * rc_ref[RET_HEADS + hd] for hd in range(RET_HEADS)], axis=1)).astype(BF16)


def _mix_fwd(cv, qkv, dtr, l, wg, bg, lam, dtb, alog, dsk):
    b, s, _ = cv.shape
    tb = min(MIX_TOKENS, s)
    row = lambda bi, i: (bi, i, 0)
    return pl.pallas_call(
        _mixf_body,
        out_shape=jax.ShapeDtypeStruct((b, s, N_PART), BF16),
        grid=(b, s // tb),
        in_specs=[
            pl.BlockSpec((None, tb, N_CONV), row),
            pl.BlockSpec((None, tb, N_QKV), row),
            pl.BlockSpec((None, tb, LANES), row),
            _layer_spec((D_GROUP // LANES, LANES, 2 * LANES), l),
            _layer_spec((1, 2 * D_GROUP), l),
            _layer_spec((1, D_GROUP), l),
            _layer_spec((1, LANES), l),
            _layer_spec((1, LANES), l),
            _layer_spec((1, D_GROUP), l),
        ],
        out_specs=pl.BlockSpec((None, tb, N_PART), row),
        scratch_shapes=[
            pltpu.VMEM((tb, LANES), F32),
            pltpu.VMEM((1, D_GROUP), F32),
            pltpu.VMEM((SSD_GROUPS, SSD_STATE, SSD_HPG * SSD_HEAD_DIM), F32),
            pltpu.VMEM((RET_HEADS, RET_HEAD_DIM, RET_HEAD_DIM), F32),
            pltpu.VMEM((5 * RET_HEADS, CHUNK, CHUNK), F32),
        ],
        compiler_params=pltpu.CompilerParams(
            dimension_semantics=("arbitrary", "arbitrary"), vmem_limit_bytes=VMEM_LIMIT),
        name="mix_fwd",
    )(cv, qkv, dtr, wg, bg, lam, dtb, alog, dsk)


def _mixb_body(x_ref, act_ref, cv_ref, gate_ref, dtr_ref, part_ref,
               wg_ref, bg_ref, lam_ref, dtb_ref, alog_ref,
               snw_ref, rnw_ref, wo_ref, o_ref,
               dts_ref, hc_ref, sb_ref, rb_ref, rc_ref):
    i = pl.program_id(1)
    tb = act_ref.shape[0]
    tr, tc = _time_consts()
    nck = tb // CHUNK
    rows = [slice(c * CHUNK, (c + 1) * CHUNK) for c in range(nck)]
    lanes = rows

    @pl.when(i == 0)
    def _():
        hc_ref[...] = jnp.zeros(hc_ref.shape, F32)
        sb_ref[...] = jnp.zeros(sb_ref.shape, F32)
        rb_ref[...] = jnp.zeros(rb_ref.shape, F32)
        _ret_consts(rc_ref, tr, tc)

    xc = cv_ref[:, 0:D_GROUP].astype(F32)
    r_pre, i_pre = _lru_gate_matmul(xc, wg_ref)
    kv_all = []
    for c in range(nck):
        kvs = []
        for hd in range(RET_HEADS):
            k = act_ref[rows[c], K_OFF + hd * RET_HEAD_DIM:K_OFF + (hd + 1) * RET_HEAD_DIM]
            v = act_ref[rows[c], V_OFF + hd * RET_HEAD_DIM:V_OFF + (hd + 1) * RET_HEAD_DIM]
            kd = (k.astype(F32) * rc_ref[4 * RET_HEADS + hd]).astype(BF16)
            kvs.append(lax.dot_general(kd, v, _TN, preferred_element_type=F32))
        kv_all.append(kvs)
    dts_ref[...] = _softplus(dtr_ref[...] + dtb_ref[...])
    ltri = jnp.where(tr >= tc, 1.0, 0.0)
    ltri3 = jnp.concatenate([ltri, ltri, ltri], axis=1).astype(BF16)
    expand = _expand_mat(SSD_HEADS)
    a_row = _neg_exp_alog(alog_ref)
    last = CHUNK - 1
    dt_all = jnp.concatenate([dts_ref[r, :] for r in rows], axis=1)
    da_all = dt_all * jnp.concatenate([a_row * LOG2E] * nck, axis=1)
    cum_all = _cumsum_rows(ltri3, da_all)
    tot_all = cum_all[last:last + 1, :]
    rcum_all = tot_all - cum_all + da_all
    e_rows = []
    for l in lanes:
        e_rows += [jnp.exp2(tot_all[:, l] - rcum_all[:, l]) * dt_all[:, l], jnp.exp2(rcum_all[:, l])]
    e2 = _dot(jnp.concatenate(e_rows, axis=0).astype(BF16), expand.astype(BF16))
    cdec = _dot_hi(jnp.concatenate([jnp.broadcast_to(jnp.exp2(tot_all[:, l]), (SUBLANES, LANES)) for l in lanes],
                                   axis=0), expand)
    st_all = []
    for c in range(nck):
        xb = cv_ref[rows[c], D_GROUP:2 * D_GROUP]
        bm = cv_ref[rows[c], 2 * D_GROUP:2 * D_GROUP + LANES]
        xd = xb * e2[2 * c * CHUNK:(2 * c + 1) * CHUNK, :].astype(BF16)
        st_all.append([lax.dot_general(bm[:, g * SSD_STATE:(g + 1) * SSD_STATE], xd[:, g * 256:(g + 1) * 256],
                                       _TN, preferred_element_type=F32) for g in range(SSD_GROUPS)])

    sp = _lru_rate(lam_ref)
    scans = [None] * nck
    y_ret = [None] * nck
    y_ssd = [None] * nck
    for c in range(nck - 1, -1, -1):
        cm = cv_ref[rows[c], 2 * D_GROUP + LANES:N_CONV]
        yo = [_dot(cm[:, g * SSD_STATE:(g + 1) * SSD_STATE], sb_ref[g].astype(BF16)) for g in range(SSD_GROUPS)]
        inter = [_dot(act_ref[rows[c], Q_OFF + hd * RET_HEAD_DIM:Q_OFF + (hd + 1) * RET_HEAD_DIM],
                      rb_ref[hd].astype(BF16)) for hd in range(RET_HEADS)]
        cdec_x = cdec[SUBLANES * c:SUBLANES * c + 1, :]
        for g in range(SSD_GROUPS):
            sb_ref[g] = sb_ref[g] * cdec_x[:, g * 256:(g + 1) * 256] + st_all[c][g]
        for hd in range(RET_HEADS):
            rb_ref[hd] = rb_ref[hd] * math.exp(_ret_gamma_log(hd) * CHUNK) + kv_all[c][hd]

        a, u = _lru_coeffs(xc[rows[c], :], r_pre[rows[c], :], i_pre[rows[c], :], bg_ref, sp)
        scans[c] = _seg_scan(a, u, reverse=True)

        ecum_x = e2[(2 * c + 1) * CHUNK:(2 * c + 2) * CHUNK, :]
        ys = part_ref[rows[c], D_GROUP:2 * D_GROUP].astype(F32) + jnp.concatenate(yo, axis=1) * ecum_x
        ys = ys * _silu(gate_ref[rows[c], D_GROUP:2 * D_GROUP].astype(F32))
        ys = ys * lax.rsqrt(jnp.mean(ys * ys, axis=-1, keepdims=True) + EPS) * snw_ref[...]
        y_ssd[c] = ys.astype(BF16)
        yr = []
        for hd in range(RET_HEADS):
            cs = slice(2 * D_GROUP + hd * RET_HEAD_DIM, 2 * D_GROUP + (hd + 1) * RET_HEAD_DIM)
            y = part_ref[rows[c], cs].astype(F32) + inter[hd] * rc_ref[3 * RET_HEADS + hd]
            mu = jnp.mean(y, axis=-1, keepdims=True)
            yc = y - mu
            var = jnp.mean(yc * yc, axis=-1, keepdims=True)
            ws = slice(hd * RET_HEAD_DIM, (hd + 1) * RET_HEAD_DIM)
            yr.append(yc * lax.rsqrt(var + EPS) * rnw_ref[:, ws] * _silu(gate_ref[rows[c], cs].astype(F32)))
        y_ret[c] = jnp.concatenate(yr, axis=1).astype(BF16)

    acc = _dot(jnp.concatenate(y_ret, axis=0), wo_ref[2 * D_GROUP:3 * D_GROUP, :])
    acc = acc + _dot(jnp.concatenate(y_ssd, axis=0), wo_ref[D_GROUP:2 * D_GROUP, :])
    h, hc_ref[...] = _lru_link(scans, hc_ref[...], reverse=True)
    y_lru = (part_ref[:, 0:D_GROUP].astype(F32) + h) * _gelu_tanh(gate_ref[:, 0:D_GROUP].astype(F32))
    o_ref[...] = x_ref[...] + (acc + _dot(y_lru.astype(BF16), wo_ref[0:D_GROUP, :]))


def _mix_bwd(x, qkv, cv, gate, dtr, part, l, wg, bg, lam, dtb, alog, snw, rnw, wo):
    b, s, _ = qkv.shape
    tb = min(MIX_TOKENS, s)
    nb = s // tb
    row = lambda bi, i: (bi, nb - 1 - i, 0)
    return pl.pallas_call(
        _mixb_body,
        out_shape=jax.ShapeDtypeStruct((b, s, D_MODEL), F32),
        grid=(b, nb),
        in_specs=[
            pl.BlockSpec((None, tb, D_MODEL), row),
            pl.BlockSpec((None, tb, N_QKV), row),
            pl.BlockSpec((None, tb, N_CONV), row),
            pl.BlockSpec((None, tb, N_GATE), row),
            pl.BlockSpec((None, tb, LANES), row),
            pl.BlockSpec((None, tb, N_PART), row),
            _layer_spec((D_GROUP // LANES, LANES, 2 * LANES), l),
            _layer_spec((1, 2 * D_GROUP), l),
            _layer_spec((1, D_GROUP), l),
            _layer_spec((1, LANES), l),
            _layer_spec((1, LANES), l),
            _layer_spec((1, D_GROUP), l),
            _layer_spec((1, D_GROUP), l),
            _layer_spec((N_PART, D_MODEL), l),
        ],
        out_specs=pl.BlockSpec((None, tb, D_MODEL), row),
        scratch_shapes=[
            pltpu.VMEM((tb, LANES), F32),
            pltpu.VMEM((1, D_GROUP), F32),
            pltpu.VMEM((SSD_GROUPS, SSD_STATE, SSD_HPG * SSD_HEAD_DIM), F32),
            pltpu.VMEM((RET_HEADS, RET_HEAD_DIM, RET_HEAD_DIM), F32),
            pltpu.VMEM((5 * RET_HEADS, CHUNK, CHUNK), F32),
        ],
        compiler_params=pltpu.CompilerParams(
            dimension_semantics=("arbitrary", "arbitrary"), vmem_limit_bytes=VMEM_LIMIT),
        name="mix_bwd",
    )(x, qkv, cv, gate, dtr, part, wg, bg, lam, dtb, alog, snw, rnw, wo)


def _to_chunk_order(x):
    b, s, d = x.shape
    return x.reshape(b, s // CHUNK, SUBLANES, SEG, d).swapaxes(2, 3).reshape(b, s, d)


def _from_chunk_order(x):
    b, s, d = x.shape
    return x.reshape(b, s // CHUNK, SEG, SUBLANES, d).swapaxes(2, 3).reshape(b, s, d)


def _pair_blocks(w):
    d, two, nb, bw, _ = w.shape
    w = w.reshape(d, two, nb // 2, 2, bw, bw)
    z = jnp.zeros_like(w[:, :, :, 0])
    top = jnp.concatenate([w[:, :, :, 0], z], axis=-1)
    bot = jnp.concatenate([z, w[:, :, :, 1]], axis=-1)
    return jnp.concatenate([top, bot], axis=-2)


def _prep(p):
    w_in = p['w_in']
    o = [0, 512, 1024, 1536, 2304, 2320, 2832, 3344, 3856, 4368]
    lru_x, lru_gate, ssd_z, xbc, dtc, q, k, v, g = [w_in[:, :, o[j]:o[j + 1]] for j in range(9)]
    depth = w_in.shape[0]
    row = lambda a: a.reshape(depth, 1, -1)
    pad = lambda a: jnp.pad(a.reshape(depth, -1), ((0, 0), (0, LANES - a.shape[1] * a.shape[2])))[:, None, :]
    gate_w = jnp.concatenate([_pair_blocks(p['lru_w_a']), _pair_blocks(p['lru_w_i'])], axis=-1).astype(BF16)
    gate_b = jnp.concatenate([p['lru_b_a'], p['lru_b_i']], axis=-1)[:, :, None, :]
    return dict(
        ffn1_nw=row(p['ffn1_norm']), ffn1_wgu=p['ffn1_w_gu'].astype(BF16), ffn1_wd=p['ffn1_w_down'].astype(BF16),
        ffn2_nw=row(p['ffn2_norm']), ffn2_wgu=p['ffn2_w_gu'].astype(BF16), ffn2_wd=p['ffn2_w_down'].astype(BF16),
        mix_nw=row(p['mix_norm']),
        wc=jnp.concatenate([lru_x, xbc], axis=2).astype(BF16),
        wq=jnp.concatenate([q, k, v], axis=2).astype(BF16),
        wg=jnp.concatenate([lru_gate, ssd_z, g], axis=2).astype(BF16),
        wdt=jnp.pad(dtc, ((0, 0), (0, 0), (0, LANES - dtc.shape[2]))).astype(BF16),
        cw=jnp.concatenate([p['lru_conv_w'], p['ssd_conv_w']], axis=2),
        cb=row(jnp.concatenate([p['lru_conv_b'], p['ssd_conv_b']], axis=1)),
        gate_w=[gate_w[:, d] for d in range(2)], gate_b=[gate_b[:, d] for d in range(2)],
        lam=[p['lru_lam'][:, d][:, None, :] for d in range(2)],
        dtb=pad(p['ssd_dt_bias']), alog=pad(p['ssd_a_log']),
        dsk=row(jnp.repeat(p['ssd_d'], SSD_HEAD_DIM, axis=1)),
        snw=row(p['ssd_norm']), rnw=row(p['ret_norm']),
        wo=p['w_out'].astype(BF16),
    )


def _rope_tables(s):
    d = RET_HEAD_DIM
    inv_freq = 1.0 / (ROPE_BASE ** (jnp.arange(0, d, 2, dtype=F32) / d))
    r = jnp.arange(s, dtype=jnp.int32)
    local = r % CHUNK
    pos = (r // CHUNK) * CHUNK + (local % SUBLANES) * SEG + local // SUBLANES
    ang = pos.astype(F32)[:, None] * inv_freq[None, :]
    cos = jnp.cos(ang)
    sin = jnp.sin(ang)
    return jnp.concatenate([cos, cos], axis=1), jnp.concatenate([-sin, sin], axis=1)


def _trunk(x, pp, depth, final_nw):
    _, s, _ = x.shape
    c2, s2 = _rope_tables(s)
    x = _to_chunk_order(x)
    for l in range(depth):
        x = _ffn(x, l, pp['ffn1_nw'], pp['ffn1_wgu'], pp['ffn1_wd'])
        cv, qkv, gate, dtr = _proj(x, l, pp['mix_nw'], pp['wc'], pp['wq'], pp['wg'], pp['wdt'], c2, s2, pp['cw'], pp['cb'])
        part = _mix_fwd(cv, qkv, dtr, l, pp['gate_w'][0], pp['gate_b'][0], pp['lam'][0], pp['dtb'], pp['alog'], pp['dsk'])
        x = _mix_bwd(x, qkv, cv, gate, dtr, part, l, pp['gate_w'][1], pp['gate_b'][1], pp['lam'][1],
                     pp['dtb'], pp['alog'], pp['snw'], pp['rnw'], pp['wo'])
        x = _ffn(x, l, pp['ffn2_nw'], pp['ffn2_wgu'], pp['ffn2_wd'], final_nw if l == depth - 1 else None)
    return _from_chunk_order(x)


def kernel(x_prompt, x_sample, ffn1_norm, ffn1_w_gu, ffn1_w_down, mix_norm, w_in, lru_conv_w, lru_conv_b, lru_w_a, lru_b_a, lru_w_i, lru_b_i, lru_lam, ssd_conv_w, ssd_conv_b, ssd_dt_bias, ssd_a_log, ssd_d, ssd_norm, ret_norm, w_out, ffn2_norm, ffn2_w_gu, ffn2_w_down, final_norm):
    p = dict(ffn1_norm=ffn1_norm, ffn1_w_gu=ffn1_w_gu, ffn1_w_down=ffn1_w_down, mix_norm=mix_norm, w_in=w_in,
             lru_conv_w=lru_conv_w, lru_conv_b=lru_conv_b, lru_w_a=lru_w_a, lru_b_a=lru_b_a, lru_w_i=lru_w_i,
             lru_b_i=lru_b_i, lru_lam=lru_lam, ssd_conv_w=ssd_conv_w, ssd_conv_b=ssd_conv_b, ssd_dt_bias=ssd_dt_bias,
             ssd_a_log=ssd_a_log, ssd_d=ssd_d, ssd_norm=ssd_norm, ret_norm=ret_norm, w_out=w_out,
             ffn2_norm=ffn2_norm, ffn2_w_gu=ffn2_w_gu, ffn2_w_down=ffn2_w_down)
    pp = _prep(p)
    depth = ffn1_norm.shape[0]
    final_nw = final_norm[None, :]
    return (_trunk(x_prompt, pp, depth, final_nw), _trunk(x_sample, pp, depth, final_nw))
```

```python
import math

import jax
import jax.numpy as jnp
from jax import lax
from jax.experimental import pallas as pl
from jax.experimental.pallas import tpu as pltpu

F32 = jnp.float32
BF16 = jnp.bfloat16

D_MODEL = 1024
D_FF = 2816
D_GROUP = 512
EPS = 1e-6
CONV_W = 4
LRU_C = 8.0
SSD_HEADS = 8
SSD_HEAD_DIM = 64
SSD_STATE = 64
SSD_GROUPS = 2
SSD_HPG = SSD_HEADS // SSD_GROUPS
SSD_XBC = D_GROUP + 2 * SSD_GROUPS * SSD_STATE
RET_HEADS = 4
RET_HEAD_DIM = 128
CHUNK = 128
ROPE_BASE = 10000.0

LANES = 128
SUBLANES = 8
BF16_SUBLANES = 16
SEG = CHUNK // SUBLANES
HALO = BF16_SUBLANES
VMEM_LIMIT = 56 * 1024 * 1024

N_CONV = D_GROUP + SSD_XBC
N_QKV = 3 * D_GROUP
N_GATE = 3 * D_GROUP
Q_OFF = 0
K_OFF = D_GROUP
V_OFF = 2 * D_GROUP
N_PART = 3 * D_GROUP

FFN_TOKENS = 1024
PROJ_TOKENS = 1024
MIX_TOKENS = 1024
FF_CHUNK = 512
TINY = 1e-30
LOG2E = math.log2(math.e)

_NT = (((1,), (1,)), ((), ()))
_TN = (((0,), (0,)), ((), ()))


def _dot(a, b):
    return jnp.dot(a, b, preferred_element_type=F32)


def _dot_hi(a, b):
    return jnp.dot(a, b, preferred_element_type=F32, precision=lax.Precision.HIGHEST)


def _rms(x, w):
    return x * lax.rsqrt(jnp.mean(x * x, axis=-1, keepdims=True) + EPS) * w


def _sigmoid(x):
    return 0.5 * jnp.tanh(0.5 * x) + 0.5


def _silu(x):
    h = 0.5 * x
    return h * jnp.tanh(h) + h


def _cumsum_rows(tri3, x):
    hi = x.astype(BF16)
    r1 = x - hi.astype(F32)
    mid = r1.astype(BF16)
    lo = (r1 - mid.astype(F32)).astype(BF16)
    return _dot(tri3, jnp.concatenate([hi, mid, lo], axis=0))


def _softplus(x):
    return jnp.maximum(x, 0.0) + jnp.log(1.0 + jnp.exp(-jnp.abs(x)))


def _gelu_tanh(x):
    c = math.sqrt(2.0 / math.pi)
    return 0.5 * x * (1.0 + jnp.tanh(c * (x + 0.044715 * (x * x * x))))


def _layer_spec(shape, l, **kw):
    zeros = (0,) * len(shape)
    return pl.BlockSpec((None,) + tuple(shape), lambda *_: (l,) + zeros, **kw)


def _ffn_body(x_ref, nw_ref, wgu_ref, wd_ref, *rest):
    o_ref = rest[-1]
    x = x_ref[...]
    xb = _rms(x, nw_ref[...]).astype(BF16)
    acc = jnp.zeros(x.shape, F32)
    for c in range(0, D_FF, FF_CHUNK):
        w = min(FF_CHUNK, D_FF - c)
        g = _dot(xb, wgu_ref[:, c:c + w])
        u = _dot(xb, wgu_ref[:, D_FF + c:D_FF + c + w])
        a = (_silu(g) * u).astype(BF16)
        acc = acc + _dot(a, wd_ref[c:c + w, :])
    y = x + 0.5 * acc
    o_ref[...] = _rms(y, rest[0][...]) if len(rest) == 2 else y


def _ffn(x, l, nw, wgu, wd, final_nw=None):
    b, s, _ = x.shape
    tm = min(FFN_TOKENS, s)
    row = lambda bi, i: (bi, i, 0)
    extra = [] if final_nw is None else [final_nw]
    return pl.pallas_call(
        _ffn_body,
        out_shape=jax.ShapeDtypeStruct(x.shape, F32),
        grid=(b, s // tm),
        in_specs=[pl.BlockSpec((None, tm, D_MODEL), row),
                  _layer_spec((1, D_MODEL), l),
                  _layer_spec((D_MODEL, 2 * D_FF), l, pipeline_mode=pl.Buffered(1)),
                  _layer_spec((D_FF, D_MODEL), l, pipeline_mode=pl.Buffered(1))]
        + [pl.BlockSpec((1, D_MODEL), lambda bi, i: (0, 0))] * len(extra),
        out_specs=pl.BlockSpec((None, tm, D_MODEL), row),
        compiler_params=pltpu.CompilerParams(
            dimension_semantics=("arbitrary", "arbitrary"), vmem_limit_bytes=VMEM_LIMIT),
        name="ffn",
    )(x, nw, wgu, wd, *extra)


def _proj_body(x_ref, xp_ref, xn_ref, nw_ref, wc_ref, wq_ref, wg_ref, wdt_ref, c2_ref, s2_ref, cw_ref, cb_ref,
               cv_ref, qkv_ref, g_ref, dt_ref):
    i = pl.program_id(1)
    n = pl.num_programs(1)
    tm = x_ref.shape[0]
    nw = nw_ref[...]
    xb = _rms(x_ref[...], nw).astype(BF16)
    xp = _rms(xp_ref[...] * jnp.where(i == 0, 0.0, 1.0), nw).astype(BF16)
    xn = _rms(xn_ref[...] * jnp.where(i == n - 1, 0.0, 1.0), nw).astype(BF16)
    pc = _dot(jnp.concatenate([xp, xb, xn], axis=0), wc_ref[...])
    pq = _dot(xb, wq_ref[...])
    pg = _dot(xb, wg_ref[...])
    g_ref[:, 0:D_GROUP] = _gelu_tanh(pg[:, 0:D_GROUP]).astype(BF16)
    g_ref[:, D_GROUP:N_GATE] = _silu(pg[:, D_GROUP:N_GATE]).astype(BF16)
    dt_ref[...] = _dot(xb, wdt_ref[...])
    raw = pc[HALO:HALO + tm, :]
    n1, p1, p2 = _conv_edges(raw, pc[0:HALO, :], pc[HALO + tm:HALO + tm + HALO, :])
    for c in range(tm // CHUNK):
        rows = slice(c * CHUNK, (c + 1) * CHUNK)
        seg = slice(c * SUBLANES, (c + 1) * SUBLANES)
        conv = _conv_chunk(raw[rows, :], n1[seg, :], p1[seg, :], p2[seg, :], cw_ref, cb_ref)
        cv_ref[rows, 0:D_GROUP] = conv[:, 0:D_GROUP].astype(BF16)
        cv_ref[rows, D_GROUP:N_CONV] = _silu(conv[:, D_GROUP:N_CONV]).astype(BF16)
    qkv_ref[:, V_OFF:N_QKV] = pq[:, V_OFF:N_QKV].astype(BF16)
    c2 = c2_ref[...]
    s2 = s2_ref[...]
    kscale = RET_HEAD_DIM ** -0.5
    for h in range(2 * RET_HEADS):
        lo = h * RET_HEAD_DIM
        blk = pq[:, lo:lo + RET_HEAD_DIM]
        rot = blk * c2 + pltpu.roll(blk, RET_HEAD_DIM // 2, 1) * s2
        if h >= RET_HEADS:
            rot = rot * kscale
        qkv_ref[:, lo:lo + RET_HEAD_DIM] = rot.astype(BF16)


def _proj(x, l, nw, wc, wq, wg, wdt, c2, s2, cw, cb):
    b, s, _ = x.shape
    tm = min(PROJ_TOKENS, s)
    per = tm // HALO
    last = s // HALO - 1
    row = lambda bi, i: (bi, i, 0)
    return pl.pallas_call(
        _proj_body,
        out_shape=(jax.ShapeDtypeStruct((b, s, N_CONV), BF16),
                   jax.ShapeDtypeStruct((b, s, N_QKV), BF16),
                   jax.ShapeDtypeStruct((b, s, N_GATE), BF16),
                   jax.ShapeDtypeStruct((b, s, LANES), F32)),
        grid=(b, s // tm),
        in_specs=[
            pl.BlockSpec((None, tm, D_MODEL), row),
            pl.BlockSpec((None, HALO, D_MODEL), lambda bi, i: (bi, jnp.maximum(i * per - 1, 0), 0)),
            pl.BlockSpec((None, HALO, D_MODEL), lambda bi, i: (bi, jnp.minimum((i + 1) * per, last), 0)),
            _layer_spec((1, D_MODEL), l),
            _layer_spec((D_MODEL, N_CONV), l, pipeline_mode=pl.Buffered(1)),
            _layer_spec((D_MODEL, N_QKV), l, pipeline_mode=pl.Buffered(1)),
            _layer_spec((D_MODEL, N_GATE), l, pipeline_mode=pl.Buffered(1)),
            _layer_spec((D_MODEL, LANES), l, pipeline_mode=pl.Buffered(1)),
            pl.BlockSpec((tm, LANES), lambda bi, i: (i, 0)),
            pl.BlockSpec((tm, LANES), lambda bi, i: (i, 0)),
            _layer_spec((CONV_W, N_CONV), l),
            _layer_spec((1, N_CONV), l),
        ],
        out_specs=(pl.BlockSpec((None, tm, N_CONV), row),
                   pl.BlockSpec((None, tm, N_QKV), row),
                   pl.BlockSpec((None, tm, N_GATE), row),
                   pl.BlockSpec((None, tm, LANES), row)),
        compiler_params=pltpu.CompilerParams(
            dimension_semantics=("arbitrary", "arbitrary"), vmem_limit_bytes=VMEM_LIMIT),
        name="mix_proj",
    )(x, x, x, nw, wc, wq, wg, wdt, c2, s2, cw, cb)


def _split(x):
    return x.reshape(x.shape[0] // CHUNK, SEG, SUBLANES, x.shape[1])


def _seg_rows(x4, j):
    return x4[:, j].reshape(-1, x4.shape[3])


def _conv_edges(x, prev, nxt):
    x4 = _split(x)
    nck, _, _, ch = x4.shape
    nseg = nck * SUBLANES
    rows = lax.broadcasted_iota(jnp.int32, (nseg, ch), 0)
    n1 = jnp.where(rows == nseg - 1, nxt[0:1, :], pltpu.roll(_seg_rows(x4, 0), nseg - 1, 0))
    p1 = jnp.where(rows == 0, prev[HALO - 1:HALO, :], pltpu.roll(_seg_rows(x4, SEG - 1), 1, 0))
    p2 = jnp.where(rows == 0, prev[HALO - 1 - SUBLANES:HALO - SUBLANES, :],
                   pltpu.roll(_seg_rows(x4, SEG - 2), 1, 0))
    return n1, p1, p2


def _conv_chunk(x, n1, p1, p2, cw_ref, cb_ref):
    x3 = x.reshape(SEG, SUBLANES, x.shape[1])
    xp1 = jnp.concatenate([x3[1:SEG], n1[None]], axis=0)
    xm1 = jnp.concatenate([p1[None], x3[0:SEG - 1]], axis=0)
    xm2 = jnp.concatenate([p2[None], p1[None], x3[0:SEG - 2]], axis=0)
    w = cw_ref[...]
    out = cb_ref[...] + xm2 * w[0:1, :] + xm1 * w[1:2, :] + x3 * w[2:3, :] + xp1 * w[3:4, :]
    return out.reshape(x.shape)


def _lru_gate_matmul(xc, wg_ref):
    xb = xc.astype(BF16)
    rs = []
    gs = []
    for p in range(D_GROUP // LANES):
        g = _dot(xb[:, p * LANES:(p + 1) * LANES], wg_ref[p])
        rs.append(g[:, 0:LANES])
        gs.append(g[:, LANES:2 * LANES])
    return jnp.concatenate(rs, axis=1), jnp.concatenate(gs, axis=1)


def _lru_rate(lam_ref):
    return _softplus(-lam_ref[...]) * (-LRU_C * LOG2E)


def _lru_coeffs(xc, r_pre, i_pre, bg_ref, rate):
    r = _sigmoid(r_pre + bg_ref[:, 0:D_GROUP])
    ig = _sigmoid(i_pre + bg_ref[:, D_GROUP:2 * D_GROUP])
    a = jnp.exp2(r * rate)
    om = 1.0 - a * a
    u = om * lax.rsqrt(jnp.maximum(om, TINY)) * (ig * xc)
    return a, u


def _row_scan(a, u, reverse):
    n = a.shape[0]
    rows = lax.broadcasted_iota(jnp.int32, a.shape, 0)
    s = 1
    while s < n:
        if reverse:
            a_s = pltpu.roll(a, n - s, 0)
            u_s = pltpu.roll(u, n - s, 0)
            m = rows < n - s
        else:
            a_s = pltpu.roll(a, s, 0)
            u_s = pltpu.roll(u, s, 0)
            m = rows >= s
        u = jnp.where(m, a * u_s + u, u)
        a = jnp.where(m, a * a_s, a)
        s *= 2
    return a, u


def _seg_scan(a, u, reverse):
    a3 = a.reshape(SEG, SUBLANES, a.shape[1])
    u3 = u.reshape(SEG, SUBLANES, u.shape[1])
    acc_a = [None] * SEG
    acc_u = [None] * SEG
    pa = pu = None
    for j in (range(SEG - 1, -1, -1) if reverse else range(SEG)):
        aj = a3[j]
        uj = u3[j]
        if pa is not None:
            uj = aj * pu + uj
            aj = aj * pa
        acc_a[j], acc_u[j] = aj, uj
        pa, pu = aj, uj
    return acc_a, acc_u


def _lru_link(scans, carry, reverse):
    end = 0 if reverse else SEG - 1
    tot_a = jnp.concatenate([sc[0][end] for sc in scans], axis=0)
    tot_u = jnp.concatenate([sc[1][end] for sc in scans], axis=0)
    nseg = tot_a.shape[0]
    pa, hend = _row_scan(tot_a, tot_u, reverse)
    hend = hend + pa * carry
    rows = lax.broadcasted_iota(jnp.int32, hend.shape, 0)
    if reverse:
        cin = jnp.where(rows == nseg - 1, carry, pltpu.roll(hend, nseg - 1, 0))
        new_carry = hend[0:1, :]
    else:
        cin = jnp.where(rows == 0, carry, pltpu.roll(hend, 1, 0))
        new_carry = hend[nseg - 1:nseg, :]
    hs = []
    for c, (acc_a, acc_u) in enumerate(scans):
        cc = cin[c * SUBLANES:(c + 1) * SUBLANES, :]
        hs += [acc_u[j] + acc_a[j] * cc for j in range(SEG)]
    return jnp.concatenate(hs, axis=0), new_carry


def _time_consts():
    r = lax.broadcasted_iota(jnp.int32, (CHUNK, CHUNK), 0)
    c = lax.broadcasted_iota(jnp.int32, (CHUNK, CHUNK), 1)
    tr = (r % SUBLANES) * SEG + r // SUBLANES
    tc = (c % SUBLANES) * SEG + c // SUBLANES
    return tr, tc


def _expand_mat(first_row):
    r = lax.broadcasted_iota(jnp.int32, (LANES, D_GROUP), 0)
    c = lax.broadcasted_iota(jnp.int32, (LANES, D_GROUP), 1)
    return jnp.where(r == first_row + c // SSD_HEAD_DIM, 1.0, 0.0)


def _ret_gamma_log(h):
    return math.log1p(-(2.0 ** (-5.0 - h)))


def _neg_exp_alog(alog_ref):
    lane = lax.broadcasted_iota(jnp.int32, (1, LANES), 1)
    return jnp.where(lane < 2 * SSD_HEADS, -jnp.exp(alog_ref[...]), 0.0)


def _ret_consts(rc_ref, tr, tc):
    dist = jnp.abs(tr - tc).astype(F32)
    pos = tr.astype(F32)
    for hd in range(RET_HEADS):
        lg = _ret_gamma_log(hd)
        rc_ref[hd] = jnp.exp(lg * dist)
        rc_ref[RET_HEADS + hd] = jnp.exp(lg * (pos + 1.0))
        rc_ref[2 * RET_HEADS + hd] = jnp.exp(lg * (CHUNK - 1.0 - pos))
        rc_ref[3 * RET_HEADS + hd] = jnp.exp(lg * (CHUNK - pos))
        rc_ref[4 * RET_HEADS + hd] = jnp.exp(lg * pos)


def _retention_scores(act_ref, rc_ref, rows):
    zeros_h = jnp.zeros((CHUNK, RET_HEAD_DIM), BF16)
    scs = []
    kvs = []
    for hp in range(RET_HEADS // 2):
        lo = 2 * hp * RET_HEAD_DIM
        q2 = act_ref[rows, Q_OFF + lo:Q_OFF + lo + 2 * RET_HEAD_DIM]
        k2 = act_ref[rows, K_OFF + lo:K_OFF + lo + 2 * RET_HEAD_DIM]
        v2 = act_ref[rows, V_OFF + lo:V_OFF + lo + 2 * RET_HEAD_DIM]
        ka, kb = k2[:, 0:RET_HEAD_DIM], k2[:, RET_HEAD_DIM:]
        kdiag = jnp.concatenate([jnp.concatenate([ka, zeros_h], axis=1),
                                 jnp.concatenate([zeros_h, kb], axis=1)], axis=0)
        sc = lax.dot_general(q2, kdiag, _NT, preferred_element_type=F32)
        dec = jnp.concatenate([rc_ref[2 * hp], rc_ref[2 * hp + 1]], axis=1)
        scs.append((sc * dec).astype(BF16))
        for j, kk in enumerate((ka, kb)):
            kd = (kk.astype(F32) * rc_ref[2 * RET_HEADS + 2 * hp + j]).astype(BF16)
            kvs.append(lax.dot_general(kd, v2[:, j * RET_HEAD_DIM:(j + 1) * RET_HEAD_DIM], _TN,
                                       preferred_element_type=F32))
    return scs, kvs


def _retention_values(act_ref, rows, scs):
    zeros_h = jnp.zeros((CHUNK, RET_HEAD_DIM), BF16)
    ys = []
    for hp in range(RET_HEADS // 2):
        lo = 2 * hp * RET_HEAD_DIM
        v2 = act_ref[rows, V_OFF + lo:V_OFF + lo + 2 * RET_HEAD_DIM]
        vdiag = jnp.concatenate([jnp.concatenate([v2[:, 0:RET_HEAD_DIM], zeros_h], axis=1),
                                 jnp.concatenate([zeros_h, v2[:, RET_HEAD_DIM:]], axis=1)], axis=0)
        ys.append(_dot(scs[hp], vdiag))
    return ys


def _mixf_body(cv_ref, act_ref, dtr_ref, wg_ref, bg_ref, lam_ref,
               dtb_ref, alog_ref, dsk_ref, part_ref,
               dts_ref, hc_ref, sf_ref, rf_ref, rc_ref):
    i = pl.program_id(1)
    tb = act_ref.shape[0]
    tr, tc = _time_consts()

    @pl.when(i == 0)
    def _():
        hc_ref[...] = jnp.zeros(hc_ref.shape, F32)
        sf_ref[...] = jnp.zeros(sf_ref.shape, F32)
        rf_ref[...] = jnp.zeros(rf_ref.shape, F32)
        _ret_consts(rc_ref, tr, tc)

    nck = tb // CHUNK
    rows = [slice(c * CHUNK, (c + 1) * CHUNK) for c in range(nck)]
    lanes = rows

    ret_sc = []
    ret_kv = []
    for c in range(nck):
        scs, kvs = _retention_scores(act_ref, rc_ref, rows[c])
        ret_sc.append(scs)
        ret_kv.append(kvs)

    xc = cv_ref[:, 0:D_GROUP].astype(F32)
    r_pre, i_pre = _lru_gate_matmul(xc, wg_ref)
    sp = _lru_rate(lam_ref)
    dts_ref[...] = _softplus(dtr_ref[...] + dtb_ref[...])

    lower = tr > tc
    diag = tr == tc
    ltri = jnp.where(tr >= tc, 1.0, 0.0)
    ltri3 = jnp.concatenate([ltri, ltri, ltri], axis=1).astype(BF16)
    utri = jnp.where(tr <= tc, 1.0, 0.0)
    expand = _expand_mat(0)
    a_row = _neg_exp_alog(alog_ref)
    last = CHUNK - 1
    lane = lax.broadcasted_iota(jnp.int32, (CHUNK, LANES), 1)
    first_head = lane < SSD_HEAD_DIM

    dt_all = jnp.concatenate([dts_ref[r, :] for r in rows], axis=1)
    da_all = dt_all * jnp.concatenate([a_row * LOG2E] * nck, axis=1)
    cum_all = _cumsum_rows(ltri3, da_all)
    tot_all = cum_all[last:last + 1, :]
    rcum_all = tot_all - cum_all + da_all
    da_t = jnp.concatenate([da_all[:, l].T[0:BF16_SUBLANES, :] for l in lanes], axis=0)
    dt_t = jnp.concatenate([dt_all[:, l].T[0:BF16_SUBLANES, :] for l in lanes], axis=0)
    cum_t = _dot_hi(da_t, utri)
    rcum_t = cum_t[:, last:last + 1] - cum_t + da_t
    ldt = jnp.log2(dt_t)
    sub_f = cum_t - ldt
    sub_b = rcum_t - ldt
    e_rows = []
    for l in lanes:
        e_rows += [jnp.exp2(tot_all[:, l] - cum_all[:, l]) * dt_all[:, l], jnp.exp2(cum_all[:, l])]
    e2 = _dot(jnp.concatenate(e_rows, axis=0).astype(BF16), expand.astype(BF16))
    cdec = _dot_hi(jnp.concatenate([jnp.broadcast_to(jnp.exp2(tot_all[:, l]), (SUBLANES, LANES)) for l in lanes],
                                   axis=0), expand)

    cbs = []
    st_all = []
    for c in range(nck):
        xb = cv_ref[rows[c], D_GROUP:2 * D_GROUP]
        bm = cv_ref[rows[c], 2 * D_GROUP:2 * D_GROUP + LANES]
        cm = cv_ref[rows[c], 2 * D_GROUP + LANES:N_CONV]
        xd = xb * e2[2 * c * CHUNK:(2 * c + 1) * CHUNK, :].astype(BF16)
        cbs.append([lax.dot_general(cm[:, g * SSD_STATE:(g + 1) * SSD_STATE], bm[:, g * SSD_STATE:(g + 1) * SSD_STATE],
                                    _NT, preferred_element_type=F32) for g in range(SSD_GROUPS)])
        st_all.append([lax.dot_general(bm[:, g * SSD_STATE:(g + 1) * SSD_STATE], xd[:, g * 256:(g + 1) * 256],
                                       _TN, preferred_element_type=F32) for g in range(SSD_GROUPS)])

    ret_intra = [_retention_values(act_ref, rows[c], ret_sc[c]) for c in range(nck)]
    y_ssd = []
    scans = []
    for c in range(nck):
        xb = cv_ref[rows[c], D_GROUP:2 * D_GROUP]
        x = xb.astype(F32)
        cum = cum_all[:, lanes[c]]
        rcum = rcum_all[:, lanes[c]]
        ldiag = jnp.log2(dt_t[16 * c:16 * c + SSD_HEADS, :] + dt_t[16 * c + SSD_HEADS:16 * (c + 1), :])
        ys = []
        for g in range(SSD_GROUPS):
            for pp in range(SSD_HPG // 2):
                ws = []
                for hd in (g * SSD_HPG + 2 * pp, g * SSD_HPG + 2 * pp + 1):
                    hb = SSD_HEADS + hd
                    rf = 16 * c + hd
                    rb = 16 * c + hb
                    z = jnp.where(lower, cum[:, hd:hd + 1] - sub_f[rf:rf + 1, :],
                                  jnp.where(diag, ldiag[hd:hd + 1, :], rcum[:, hb:hb + 1] - sub_b[rb:rb + 1, :]))
                    ws.append((cbs[c][g] * jnp.exp2(z)).astype(BF16))
                p0 = (g * SSD_HPG + 2 * pp) * SSD_HEAD_DIM
                xp = xb[:, p0:p0 + LANES]
                zero = jnp.zeros_like(xp)
                xdiag = jnp.concatenate([jnp.where(first_head, xp, zero), jnp.where(first_head, zero, xp)], axis=0)
                ys.append(_dot(jnp.concatenate(ws, axis=1), xdiag))
        y_ssd.append(jnp.concatenate(ys, axis=1) + dsk_ref[...] * x)
        a, u = _lru_coeffs(xc[rows[c], :], r_pre[rows[c], :], i_pre[rows[c], :], bg_ref, sp)
        scans.append(_seg_scan(a, u, reverse=False))

    h, hc_ref[...] = _lru_link(scans, hc_ref[...], reverse=False)
    part_ref[:, 0:D_GROUP] = h.astype(BF16)

    for c in range(nck):
        cm = cv_ref[rows[c], 2 * D_GROUP + LANES:N_CONV]
        ecum_x = e2[(2 * c + 1) * CHUNK:(2 * c + 2) * CHUNK, :]
        cdec_x = cdec[SUBLANES * c:SUBLANES * c + 1, :]
        yo = [_dot(cm[:, g * SSD_STATE:(g + 1) * SSD_STATE], sf_ref[g].astype(BF16)) for g in range(SSD_GROUPS)]
        inter = [_dot(act_ref[rows[c], Q_OFF + hd * RET_HEAD_DIM:Q_OFF + (hd + 1) * RET_HEAD_DIM],
                      rf_ref[hd].astype(BF16)) for hd in range(RET_HEADS)]
        for g in range(SSD_GROUPS):
            sf_ref[g] = sf_ref[g] * cdec_x[:, g * 256:(g + 1) * 256] + st_all[c][g]
        for hd in range(RET_HEADS):
            rf_ref[hd] = rf_ref[hd] * math.exp(_ret_gamma_log(hd) * CHUNK) + ret_kv[c][hd]
        part_ref[rows[c], D_GROUP:2 * D_GROUP] = (y_ssd[c] + jnp.concatenate(yo, axis=1) * ecum_x).astype(BF16)
        part_ref[rows[c], 2 * D_GROUP:3 * D_GROUP] = (
            jnp.concatenate(ret_intra[c], axis=1)
            + jnp.concatenate([inter[hd] * rc_ref[RET_HEADS + hd] for hd in range(RET_HEADS)], axis=1)).astype(BF16)


def _mix_fwd(cv, qkv, dtr, l, wg, bg, lam, dtb, alog, dsk):
    b, s, _ = cv.shape
    tb = min(MIX_TOKENS, s)
    row = lambda bi, i: (bi, i, 0)
    return pl.pallas_call(
        _mixf_body,
        out_shape=jax.ShapeDtypeStruct((b, s, N_PART), BF16),
        grid=(b, s // tb),
        in_specs=[
            pl.BlockSpec((None, tb, N_CONV), row),
            pl.BlockSpec((None, tb, N_QKV), row),
            pl.BlockSpec((None, tb, LANES), row),
            _layer_spec((D_GROUP // LANES, LANES, 2 * LANES), l),
            _layer_spec((1, 2 * D_GROUP), l),
            _layer_spec((1, D_GROUP), l),
            _layer_spec((1, LANES), l),
            _layer_spec((1, LANES), l),
            _layer_spec((1, D_GROUP), l),
        ],
        out_specs=pl.BlockSpec((None, tb, N_PART), row),
        scratch_shapes=[
            pltpu.VMEM((tb, LANES), F32),
            pltpu.VMEM((1, D_GROUP), F32),
            pltpu.VMEM((SSD_GROUPS, SSD_STATE, SSD_HPG * SSD_HEAD_DIM), F32),
            pltpu.VMEM((RET_HEADS, RET_HEAD_DIM, RET_HEAD_DIM), F32),
            pltpu.VMEM((5 * RET_HEADS, CHUNK, CHUNK), F32),
        ],
        compiler_params=pltpu.CompilerParams(
            dimension_semantics=("arbitrary", "arbitrary"), vmem_limit_bytes=VMEM_LIMIT),
        name="mix_fwd",
    )(cv, qkv, dtr, wg, bg, lam, dtb, alog, dsk)


def _mixb_body(x_ref, act_ref, cv_ref, gate_ref, dtr_ref, part_ref,
               wg_ref, bg_ref, lam_ref, dtb_ref, alog_ref,
               snw_ref, rnw_ref, wo_ref, o_ref,
               dts_ref, hc_ref, sb_ref, rb_ref, rc_ref):
    i = pl.program_id(1)
    tb = act_ref.shape[0]
    tr, tc = _time_consts()
    nck = tb // CHUNK
    rows = [slice(c * CHUNK, (c + 1) * CHUNK) for c in range(nck)]
    lanes = rows

    @pl.when(i == 0)
    def _():
        hc_ref[...] = jnp.zeros(hc_ref.shape, F32)
        sb_ref[...] = jnp.zeros(sb_ref.shape, F32)
        rb_ref[...] = jnp.zeros(rb_ref.shape, F32)
        _ret_consts(rc_ref, tr, tc)

    xc = cv_ref[:, 0:D_GROUP].astype(F32)
    r_pre, i_pre = _lru_gate_matmul(xc, wg_ref)
    kv_all = []
    for c in range(nck):
        kvs = []
        for hd in range(RET_HEADS):
            k = act_ref[rows[c], K_OFF + hd * RET_HEAD_DIM:K_OFF + (hd + 1) * RET_HEAD_DIM]
            v = act_ref[rows[c], V_OFF + hd * RET_HEAD_DIM:V_OFF + (hd + 1) * RET_HEAD_DIM]
            kd = (k.astype(F32) * rc_ref[4 * RET_HEADS + hd]).astype(BF16)
            kvs.append(lax.dot_general(kd, v, _TN, preferred_element_type=F32))
        kv_all.append(kvs)
    dts_ref[...] = _softplus(dtr_ref[...] + dtb_ref[...])
    ltri = jnp.where(tr >= tc, 1.0, 0.0)
    ltri3 = jnp.concatenate([ltri, ltri, ltri], axis=1).astype(BF16)
    expand = _expand_mat(SSD_HEADS)
    a_row = _neg_exp_alog(alog_ref)
    last = CHUNK - 1
    dt_all = jnp.concatenate([dts_ref[r, :] for r in rows], axis=1)
    da_all = dt_all * jnp.concatenate([a_row * LOG2E] * nck, axis=1)
    cum_all = _cumsum_rows(ltri3, da_all)
    tot_all = cum_all[last:last + 1, :]
    rcum_all = tot_all - cum_all + da_all
    e_rows = []
    for l in lanes:
        e_rows += [jnp.exp2(tot_all[:, l] - rcum_all[:, l]) * dt_all[:, l], jnp.exp2(rcum_all[:, l])]
    e2 = _dot(jnp.concatenate(e_rows, axis=0).astype(BF16), expand.astype(BF16))
    cdec = _dot_hi(jnp.concatenate([jnp.broadcast_to(jnp.exp2(tot_all[:, l]), (SUBLANES, LANES)) for l in lanes],
                                   axis=0), expand)
    st_all = []
    for c in range(nck):
        xb = cv_ref[rows[c], D_GROUP:2 * D_GROUP]
        bm = cv_ref[rows[c], 2 * D_GROUP:2 * D_GROUP + LANES]
        xd = xb * e2[2 * c * CHUNK:(2 * c + 1) * CHUNK, :].astype(BF16)
        st_all.append([lax.dot_general(bm[:, g * SSD_STATE:(g + 1) * SSD_STATE], xd[:, g * 256:(g + 1) * 256],
                                       _TN, preferred_element_type=F32) for g in range(SSD_GROUPS)])

    sp = _lru_rate(lam_ref)
    scans = [None] * nck
    y_ret = [None] * nck
    y_ssd = [None] * nck
    for c in range(nck - 1, -1, -1):
        cm = cv_ref[rows[c], 2 * D_GROUP + LANES:N_CONV]
        yo = [_dot(cm[:, g * SSD_STATE:(g + 1) * SSD_STATE], sb_ref[g].astype(BF16)) for g in range(SSD_GROUPS)]
        inter = [_dot(act_ref[rows[c], Q_OFF + hd * RET_HEAD_DIM:Q_OFF + (hd + 1) * RET_HEAD_DIM],
                      rb_ref[hd].astype(BF16)) for hd in range(RET_HEADS)]
        cdec_x = cdec[SUBLANES * c:SUBLANES * c + 1, :]
        for g in range(SSD_GROUPS):
            sb_ref[g] = sb_ref[g] * cdec_x[:, g * 256:(g + 1) * 256] + st_all[c][g]
        for hd in range(RET_HEADS):
            rb_ref[hd] = rb_ref[hd] * math.exp(_ret_gamma_log(hd) * CHUNK) + kv_all[c][hd]

        a, u = _lru_coeffs(xc[rows[c], :], r_pre[rows[c], :], i_pre[rows[c], :], bg_ref, sp)
        scans[c] = _seg_scan(a, u, reverse=True)

        ecum_x = e2[(2 * c + 1) * CHUNK:(2 * c + 2) * CHUNK, :]
        ys = part_ref[rows[c], D_GROUP:2 * D_GROUP].astype(F32) + jnp.concatenate(yo, axis=1) * ecum_x
        ys = ys * gate_ref[rows[c], D_GROUP:2 * D_GROUP].astype(F32)
        ys = ys * lax.rsqrt(jnp.mean(ys * ys, axis=-1, keepdims=True) + EPS) * snw_ref[...]
        y_ssd[c] = ys.astype(BF16)
        yr = []
        for hd in range(RET_HEADS):
            cs = slice(2 * D_GROUP + hd * RET_HEAD_DIM, 2 * D_GROUP + (hd + 1) * RET_HEAD_DIM)
            y = part_ref[rows[c], cs].astype(F32) + inter[hd] * rc_ref[3 * RET_HEADS + hd]
            mu = jnp.mean(y, axis=-1, keepdims=True)
            yc = y - mu
            var = jnp.mean(yc * yc, axis=-1, keepdims=True)
            ws = slice(hd * RET_HEAD_DIM, (hd + 1) * RET_HEAD_DIM)
            yr.append(yc * lax.rsqrt(var + EPS) * rnw_ref[:, ws] * gate_ref[rows[c], cs].astype(F32))
        y_ret[c] = jnp.concatenate(yr, axis=1).astype(BF16)

    acc = _dot(jnp.concatenate(y_ret, axis=0), wo_ref[2 * D_GROUP:3 * D_GROUP, :])
    acc = acc + _dot(jnp.concatenate(y_ssd, axis=0), wo_ref[D_GROUP:2 * D_GROUP, :])
    h, hc_ref[...] = _lru_link(scans, hc_ref[...], reverse=True)
    y_lru = (part_ref[:, 0:D_GROUP].astype(F32) + h) * gate_ref[:, 0:D_GROUP].astype(F32)
    o_ref[...] = x_ref[...] + (acc + _dot(y_lru.astype(BF16), wo_ref[0:D_GROUP, :]))


def _mix_bwd(x, qkv, cv, gate, dtr, part, l, wg, bg, lam, dtb, alog, snw, rnw, wo):
    b, s, _ = qkv.shape
    tb = min(MIX_TOKENS, s)
    nb = s // tb
    row = lambda bi, i: (bi, nb - 1 - i, 0)
    return pl.pallas_call(
        _mixb_body,
        out_shape=jax.ShapeDtypeStruct((b, s, D_MODEL), F32),
        grid=(b, nb),
        in_specs=[
            pl.BlockSpec((None, tb, D_MODEL), row),
            pl.BlockSpec((None, tb, N_QKV), row),
            pl.BlockSpec((None, tb, N_CONV), row),
            pl.BlockSpec((None, tb, N_GATE), row),
            pl.BlockSpec((None, tb, LANES), row),
            pl.BlockSpec((None, tb, N_PART), row),
            _layer_spec((D_GROUP // LANES, LANES, 2 * LANES), l),
            _layer_spec((1, 2 * D_GROUP), l),
            _layer_spec((1, D_GROUP), l),
            _layer_spec((1, LANES), l),
            _layer_spec((1, LANES), l),
            _layer_spec((1, D_GROUP), l),
            _layer_spec((1, D_GROUP), l),
            _layer_spec((N_PART, D_MODEL), l),
        ],
        out_specs=pl.BlockSpec((None, tb, D_MODEL), row),
        scratch_shapes=[
            pltpu.VMEM((tb, LANES), F32),
            pltpu.VMEM((1, D_GROUP), F32),
            pltpu.VMEM((SSD_GROUPS, SSD_STATE, SSD_HPG * SSD_HEAD_DIM), F32),
            pltpu.VMEM((RET_HEADS, RET_HEAD_DIM, RET_HEAD_DIM), F32),
            pltpu.VMEM((5 * RET_HEADS, CHUNK, CHUNK), F32),
        ],
        compiler_params=pltpu.CompilerParams(
            dimension_semantics=("arbitrary", "arbitrary"), vmem_limit_bytes=VMEM_LIMIT),
        name="mix_bwd",
    )(x, qkv, cv, gate, dtr, part, wg, bg, lam, dtb, alog, snw, rnw, wo)


def _to_chunk_order(x):
    b, s, d = x.shape
    return x.reshape(b, s // CHUNK, SUBLANES, SEG, d).swapaxes(2, 3).reshape(b, s, d)


def _from_chunk_order(x):
    b, s, d = x.shape
    return x.reshape(b, s // CHUNK, SEG, SUBLANES, d).swapaxes(2, 3).reshape(b, s, d)


def _pair_blocks(w):
    d, two, nb, bw, _ = w.shape
    w = w.reshape(d, two, nb // 2, 2, bw, bw)
    z = jnp.zeros_like(w[:, :, :, 0])
    top = jnp.concatenate([w[:, :, :, 0], z], axis=-1)
    bot = jnp.concatenate([z, w[:, :, :, 1]], axis=-1)
    return jnp.concatenate([top, bot], axis=-2)


def _prep(p):
    w_in = p['w_in']
    o = [0, 512, 1024, 1536, 2304, 2320, 2832, 3344, 3856, 4368]
    lru_x, lru_gate, ssd_z, xbc, dtc, q, k, v, g = [w_in[:, :, o[j]:o[j + 1]] for j in range(9)]
    depth = w_in.shape[0]
    row = lambda a: a.reshape(depth, 1, -1)
    pad = lambda a: jnp.pad(a.reshape(depth, -1), ((0, 0), (0, LANES - a.shape[1] * a.shape[2])))[:, None, :]
    gate_w = jnp.concatenate([_pair_blocks(p['lru_w_a']), _pair_blocks(p['lru_w_i'])], axis=-1).astype(BF16)
    gate_b = jnp.concatenate([p['lru_b_a'], p['lru_b_i']], axis=-1)[:, :, None, :]
    return dict(
        ffn1_nw=row(p['ffn1_norm']), ffn1_wgu=p['ffn1_w_gu'].astype(BF16), ffn1_wd=p['ffn1_w_down'].astype(BF16),
        ffn2_nw=row(p['ffn2_norm']), ffn2_wgu=p['ffn2_w_gu'].astype(BF16), ffn2_wd=p['ffn2_w_down'].astype(BF16),
        mix_nw=row(p['mix_norm']),
        wc=jnp.concatenate([lru_x, xbc], axis=2).astype(BF16),
        wq=jnp.concatenate([q, k, v], axis=2).astype(BF16),
        wg=jnp.concatenate([lru_gate, ssd_z, g], axis=2).astype(BF16),
        wdt=jnp.pad(dtc, ((0, 0), (0, 0), (0, LANES - dtc.shape[2]))).astype(BF16),
        cw=jnp.concatenate([p['lru_conv_w'], p['ssd_conv_w']], axis=2),
        cb=row(jnp.concatenate([p['lru_conv_b'], p['ssd_conv_b']], axis=1)),
        gate_w=[gate_w[:, d] for d in range(2)], gate_b=[gate_b[:, d] for d in range(2)],
        lam=[p['lru_lam'][:, d][:, None, :] for d in range(2)],
        dtb=pad(p['ssd_dt_bias']), alog=pad(p['ssd_a_log']),
        dsk=row(jnp.repeat(p['ssd_d'], SSD_HEAD_DIM, axis=1)),
        snw=row(p['ssd_norm']), rnw=row(p['ret_norm']),
        wo=p['w_out'].astype(BF16),
    )


def _rope_tables(s):
    d = RET_HEAD_DIM
    inv_freq = 1.0 / (ROPE_BASE ** (jnp.arange(0, d, 2, dtype=F32) / d))
    r = jnp.arange(s, dtype=jnp.int32)
    local = r % CHUNK
    pos = (r // CHUNK) * CHUNK + (local % SUBLANES) * SEG + local // SUBLANES
    ang = pos.astype(F32)[:, None] * inv_freq[None, :]
    cos = jnp.cos(ang)
    sin = jnp.sin(ang)
    return jnp.concatenate([cos, cos], axis=1), jnp.concatenate([-sin, sin], axis=1)


def _trunk(x, pp, depth, final_nw):
    _, s, _ = x.shape
    c2, s2 = _rope_tables(s)
    x = _to_chunk_order(x)
    for l in range(depth):
        x = _ffn(x, l, pp['ffn1_nw'], pp['ffn1_wgu'], pp['ffn1_wd'])
        cv, qkv, gate, dtr = _proj(x, l, pp['mix_nw'], pp['wc'], pp['wq'], pp['wg'], pp['wdt'], c2, s2, pp['cw'], pp['cb'])
        part = _mix_fwd(cv, qkv, dtr, l, pp['gate_w'][0], pp['gate_b'][0], pp['lam'][0], pp['dtb'], pp['alog'], pp['dsk'])
        x = _mix_bwd(x, qkv, cv, gate, dtr, part, l, pp['gate_w'][1], pp['gate_b'][1], pp['lam'][1],
                     pp['dtb'], pp['alog'], pp['snw'], pp['rnw'], pp['wo'])
        x = _ffn(x, l, pp['ffn2_nw'], pp['ffn2_wgu'], pp['ffn2_wd'], final_nw if l == depth - 1 else None)
    return _from_chunk_order(x)


def kernel(x_prompt, x_sample, ffn1_norm, ffn1_w_gu, ffn1_w_down, mix_norm, w_in, lru_conv_w, lru_conv_b, lru_w_a, lru_b_a, lru_w_i, lru_b_i, lru_lam, ssd_conv_w, ssd_conv_b, ssd_dt_bias, ssd_a_log, ssd_d, ssd_norm, ret_norm, w_out, ffn2_norm, ffn2_w_gu, ffn2_w_down, final_norm):
    p = dict(ffn1_norm=ffn1_norm, ffn1_w_gu=ffn1_w_gu, ffn1_w_down=ffn1_w_down, mix_norm=mix_norm, w_in=w_in,
             lru_conv_w=lru_conv_w, lru_conv_b=lru_conv_b, lru_w_a=lru_w_a, lru_b_a=lru_b_a, lru_w_i=lru_w_i,
             lru_b_i=lru_b_i, lru_lam=lru_lam, ssd_conv_w=ssd_conv_w, ssd_conv_b=ssd_conv_b, ssd_dt_bias=ssd_dt_bias,
             ssd_a_log=ssd_a_log, ssd_d=ssd_d, ssd_norm=ssd_norm, ret_norm=ret_norm, w_out=w_out,
             ffn2_norm=ffn2_norm, ffn2_w_gu=ffn2_w_gu, ffn2_w_down=ffn2_w_down)
    pp = _prep(p)
    depth = ffn1_norm.shape[0]
    final_nw = final_norm[None, :]
    return (_trunk(x_prompt, pp, depth, final_nw), _trunk(x_sample, pp, depth, final_nw))
```

```python
import math

import jax
import jax.numpy as jnp
from jax import lax
from jax.experimental import pallas as pl
from jax.experimental.pallas import tpu as pltpu

F32 = jnp.float32
BF16 = jnp.bfloat16

D_MODEL = 1024
D_FF = 2816
D_GROUP = 512
EPS = 1e-6
CONV_W = 4
LRU_C = 8.0
SSD_HEADS = 8
SSD_HEAD_DIM = 64
SSD_STATE = 64
SSD_GROUPS = 2
SSD_HPG = SSD_HEADS // SSD_GROUPS
SSD_XBC = D_GROUP + 2 * SSD_GROUPS * SSD_STATE
RET_HEADS = 4
RET_HEAD_DIM = 128
CHUNK = 128
ROPE_BASE = 10000.0

LANES = 128
SUBLANES = 8
BF16_SUBLANES = 16
SEG = CHUNK // SUBLANES
HALO = BF16_SUBLANES
VMEM_LIMIT = 56 * 1024 * 1024

N_CONV = D_GROUP + SSD_XBC
N_QKV = 3 * D_GROUP
N_GATE = 3 * D_GROUP
Q_OFF = 0
K_OFF = D_GROUP
V_OFF = 2 * D_GROUP
N_PART = 3 * D_GROUP

FFN_TOKENS = 1024
PROJ_TOKENS = 1024
MIX_TOKENS = 1024
FF_CHUNK = 512
TINY = 1e-30
LOG2E = math.log2(math.e)

_NT = (((1,), (1,)), ((), ()))
_TN = (((0,), (0,)), ((), ()))


def _dot(a, b):
    return jnp.dot(a, b, preferred_element_type=F32)


def _dot_hi(a, b):
    return jnp.dot(a, b, preferred_element_type=F32, precision=lax.Precision.HIGHEST)


def _rms(x, w):
    return x * lax.rsqrt(jnp.mean(x * x, axis=-1, keepdims=True) + EPS) * w


def _sigmoid(x):
    return 0.5 * jnp.tanh(0.5 * x) + 0.5


def _silu(x):
    h = 0.5 * x
    return h * jnp.tanh(h) + h


def _cumsum_rows(tri3, x):
    hi = x.astype(BF16)
    r1 = x - hi.astype(F32)
    mid = r1.astype(BF16)
    lo = (r1 - mid.astype(F32)).astype(BF16)
    return _dot(tri3, jnp.concatenate([hi, mid, lo], axis=0))


def _softplus(x):
    return jnp.maximum(x, 0.0) + jnp.log(1.0 + jnp.exp(-jnp.abs(x)))


def _gelu_tanh(x):
    c = math.sqrt(2.0 / math.pi)
    return 0.5 * x * (1.0 + jnp.tanh(c * (x + 0.044715 * (x * x * x))))


def _layer_spec(shape, l, **kw):
    zeros = (0,) * len(shape)
    return pl.BlockSpec((None,) + tuple(shape), lambda *_: (l,) + zeros, **kw)


def _ffn_body(x_ref, nw_ref, wgu_ref, wd_ref, *rest):
    o_ref = rest[-1]
    x = x_ref[...]
    xb = _rms(x, nw_ref[...]).astype(BF16)
    acc = jnp.zeros(x.shape, F32)
    for c in range(0, D_FF, FF_CHUNK):
        w = min(FF_CHUNK, D_FF - c)
        g = _dot(xb, wgu_ref[:, c:c + w])
        u = _dot(xb, wgu_ref[:, D_FF + c:D_FF + c + w])
        a = (_silu(g) * u).astype(BF16)
        acc = acc + _dot(a, wd_ref[c:c + w, :])
    y = x + 0.5 * acc
    o_ref[...] = _rms(y, rest[0][...]) if len(rest) == 2 else y


def _ffn(x, l, nw, wgu, wd, final_nw=None):
    b, s, _ = x.shape
    tm = min(FFN_TOKENS, s)
    row = lambda bi, i: (bi, i, 0)
    extra = [] if final_nw is None else [final_nw]
    return pl.pallas_call(
        _ffn_body,
        out_shape=jax.ShapeDtypeStruct(x.shape, F32),
        grid=(b, s // tm),
        in_specs=[pl.BlockSpec((None, tm, D_MODEL), row),
                  _layer_spec((1, D_MODEL), l),
                  _layer_spec((D_MODEL, 2 * D_FF), l, pipeline_mode=pl.Buffered(1)),
                  _layer_spec((D_FF, D_MODEL), l, pipeline_mode=pl.Buffered(1))]
        + [pl.BlockSpec((1, D_MODEL), lambda bi, i: (0, 0))] * len(extra),
        out_specs=pl.BlockSpec((None, tm, D_MODEL), row),
        compiler_params=pltpu.CompilerParams(
            dimension_semantics=("arbitrary", "arbitrary"), vmem_limit_bytes=VMEM_LIMIT),
        name="ffn",
    )(x, nw, wgu, wd, *extra)


def _proj_body(x_ref, xp_ref, xn_ref, nw_ref, wc_ref, wq_ref, wg_ref, wdt_ref, c2_ref, s2_ref, cw_ref, cb_ref,
               dtb_ref, cv_ref, qkv_ref, g_ref, dt_ref):
    i = pl.program_id(1)
    n = pl.num_programs(1)
    tm = x_ref.shape[0]
    nw = nw_ref[...]
    xb = _rms(x_ref[...], nw).astype(BF16)
    xp = _rms(xp_ref[...] * jnp.where(i == 0, 0.0, 1.0), nw).astype(BF16)
    xn = _rms(xn_ref[...] * jnp.where(i == n - 1, 0.0, 1.0), nw).astype(BF16)
    pc = _dot(jnp.concatenate([xp, xb, xn], axis=0), wc_ref[...])
    pq = _dot(xb, wq_ref[...])
    pg = _dot(xb, wg_ref[...])
    g_ref[:, 0:D_GROUP] = _gelu_tanh(pg[:, 0:D_GROUP]).astype(BF16)
    g_ref[:, D_GROUP:N_GATE] = _silu(pg[:, D_GROUP:N_GATE]).astype(BF16)
    dt_ref[...] = _softplus(_dot(xb, wdt_ref[...]) + dtb_ref[...])
    raw = pc[HALO:HALO + tm, :]
    n1, p1, p2 = _conv_edges(raw, pc[0:HALO, :], pc[HALO + tm:HALO + tm + HALO, :])
    for c in range(tm // CHUNK):
        rows = slice(c * CHUNK, (c + 1) * CHUNK)
        seg = slice(c * SUBLANES, (c + 1) * SUBLANES)
        conv = _conv_chunk(raw[rows, :], n1[seg, :], p1[seg, :], p2[seg, :], cw_ref, cb_ref)
        cv_ref[rows, 0:D_GROUP] = conv[:, 0:D_GROUP].astype(BF16)
        cv_ref[rows, D_GROUP:N_CONV] = _silu(conv[:, D_GROUP:N_CONV]).astype(BF16)
    qkv_ref[:, V_OFF:N_QKV] = pq[:, V_OFF:N_QKV].astype(BF16)
    c2 = c2_ref[...]
    s2 = s2_ref[...]
    kscale = RET_HEAD_DIM ** -0.5
    for h in range(2 * RET_HEADS):
        lo = h * RET_HEAD_DIM
        blk = pq[:, lo:lo + RET_HEAD_DIM]
        rot = blk * c2 + pltpu.roll(blk, RET_HEAD_DIM // 2, 1) * s2
        if h >= RET_HEADS:
            rot = rot * kscale
        qkv_ref[:, lo:lo + RET_HEAD_DIM] = rot.astype(BF16)


def _proj(x, l, nw, wc, wq, wg, wdt, c2, s2, cw, cb, dtb):
    b, s, _ = x.shape
    tm = min(PROJ_TOKENS, s)
    per = tm // HALO
    last = s // HALO - 1
    row = lambda bi, i: (bi, i, 0)
    return pl.pallas_call(
        _proj_body,
        out_shape=(jax.ShapeDtypeStruct((b, s, N_CONV), BF16),
                   jax.ShapeDtypeStruct((b, s, N_QKV), BF16),
                   jax.ShapeDtypeStruct((b, s, N_GATE), BF16),
                   jax.ShapeDtypeStruct((b, s, LANES), F32)),
        grid=(b, s // tm),
        in_specs=[
            pl.BlockSpec((None, tm, D_MODEL), row),
            pl.BlockSpec((None, HALO, D_MODEL), lambda bi, i: (bi, jnp.maximum(i * per - 1, 0), 0)),
            pl.BlockSpec((None, HALO, D_MODEL), lambda bi, i: (bi, jnp.minimum((i + 1) * per, last), 0)),
            _layer_spec((1, D_MODEL), l),
            _layer_spec((D_MODEL, N_CONV), l, pipeline_mode=pl.Buffered(1)),
            _layer_spec((D_MODEL, N_QKV), l, pipeline_mode=pl.Buffered(1)),
            _layer_spec((D_MODEL, N_GATE), l, pipeline_mode=pl.Buffered(1)),
            _layer_spec((D_MODEL, LANES), l, pipeline_mode=pl.Buffered(1)),
            pl.BlockSpec((tm, LANES), lambda bi, i: (i, 0)),
            pl.BlockSpec((tm, LANES), lambda bi, i: (i, 0)),
            _layer_spec((CONV_W, N_CONV), l),
            _layer_spec((1, N_CONV), l),
            _layer_spec((1, LANES), l),
        ],
        out_specs=(pl.BlockSpec((None, tm, N_CONV), row),
                   pl.BlockSpec((None, tm, N_QKV), row),
                   pl.BlockSpec((None, tm, N_GATE), row),
                   pl.BlockSpec((None, tm, LANES), row)),
        compiler_params=pltpu.CompilerParams(
            dimension_semantics=("arbitrary", "arbitrary"), vmem_limit_bytes=VMEM_LIMIT),
        name="mix_proj",
    )(x, x, x, nw, wc, wq, wg, wdt, c2, s2, cw, cb, dtb)


def _split(x):
    return x.reshape(x.shape[0] // CHUNK, SEG, SUBLANES, x.shape[1])


def _seg_rows(x4, j):
    return x4[:, j].reshape(-1, x4.shape[3])


def _conv_edges(x, prev, nxt):
    x4 = _split(x)
    nck, _, _, ch = x4.shape
    nseg = nck * SUBLANES
    rows = lax.broadcasted_iota(jnp.int32, (nseg, ch), 0)
    n1 = jnp.where(rows == nseg - 1, nxt[0:1, :], pltpu.roll(_seg_rows(x4, 0), nseg - 1, 0))
    p1 = jnp.where(rows == 0, prev[HALO - 1:HALO, :], pltpu.roll(_seg_rows(x4, SEG - 1), 1, 0))
    p2 = jnp.where(rows == 0, prev[HALO - 1 - SUBLANES:HALO - SUBLANES, :],
                   pltpu.roll(_seg_rows(x4, SEG - 2), 1, 0))
    return n1, p1, p2


def _conv_chunk(x, n1, p1, p2, cw_ref, cb_ref):
    x3 = x.reshape(SEG, SUBLANES, x.shape[1])
    xp1 = jnp.concatenate([x3[1:SEG], n1[None]], axis=0)
    xm1 = jnp.concatenate([p1[None], x3[0:SEG - 1]], axis=0)
    xm2 = jnp.concatenate([p2[None], p1[None], x3[0:SEG - 2]], axis=0)
    w = cw_ref[...]
    out = cb_ref[...] + xm2 * w[0:1, :] + xm1 * w[1:2, :] + x3 * w[2:3, :] + xp1 * w[3:4, :]
    return out.reshape(x.shape)


def _lru_gate_matmul(xc, wg_ref):
    xb = xc.astype(BF16)
    rs = []
    gs = []
    for p in range(D_GROUP // LANES):
        g = _dot(xb[:, p * LANES:(p + 1) * LANES], wg_ref[p])
        rs.append(g[:, 0:LANES])
        gs.append(g[:, LANES:2 * LANES])
    return jnp.concatenate(rs, axis=1), jnp.concatenate(gs, axis=1)


def _lru_rate(lam_ref):
    return _softplus(-lam_ref[...]) * (-LRU_C * LOG2E)


def _lru_coeffs(xc, r_pre, i_pre, bg_ref, rate):
    r = _sigmoid(r_pre + bg_ref[:, 0:D_GROUP])
    ig = _sigmoid(i_pre + bg_ref[:, D_GROUP:2 * D_GROUP])
    a = jnp.exp2(r * rate)
    om = 1.0 - a * a
    u = om * lax.rsqrt(jnp.maximum(om, TINY)) * (ig * xc)
    return a, u


def _row_scan(a, u, reverse):
    n = a.shape[0]
    rows = lax.broadcasted_iota(jnp.int32, a.shape, 0)
    s = 1
    while s < n:
        if reverse:
            a_s = pltpu.roll(a, n - s, 0)
            u_s = pltpu.roll(u, n - s, 0)
            m = rows < n - s
        else:
            a_s = pltpu.roll(a, s, 0)
            u_s = pltpu.roll(u, s, 0)
            m = rows >= s
        u = jnp.where(m, a * u_s + u, u)
        a = jnp.where(m, a * a_s, a)
        s *= 2
    return a, u


def _seg_scan(a, u, reverse):
    a3 = a.reshape(SEG, SUBLANES, a.shape[1])
    u3 = u.reshape(SEG, SUBLANES, u.shape[1])
    acc_a = [None] * SEG
    acc_u = [None] * SEG
    pa = pu = None
    for j in (range(SEG - 1, -1, -1) if reverse else range(SEG)):
        aj = a3[j]
        uj = u3[j]
        if pa is not None:
            uj = aj * pu + uj
            aj = aj * pa
        acc_a[j], acc_u[j] = aj, uj
        pa, pu = aj, uj
    return acc_a, acc_u


def _lru_link(scans, carry, reverse):
    end = 0 if reverse else SEG - 1
    tot_a = jnp.concatenate([sc[0][end] for sc in scans], axis=0)
    tot_u = jnp.concatenate([sc[1][end] for sc in scans], axis=0)
    nseg = tot_a.shape[0]
    pa, hend = _row_scan(tot_a, tot_u, reverse)
    hend = hend + pa * carry
    rows = lax.broadcasted_iota(jnp.int32, hend.shape, 0)
    if reverse:
        cin = jnp.where(rows == nseg - 1, carry, pltpu.roll(hend, nseg - 1, 0))
        new_carry = hend[0:1, :]
    else:
        cin = jnp.where(rows == 0, carry, pltpu.roll(hend, 1, 0))
        new_carry = hend[nseg - 1:nseg, :]
    hs = []
    for c, (acc_a, acc_u) in enumerate(scans):
        cc = cin[c * SUBLANES:(c + 1) * SUBLANES, :]
        hs += [acc_u[j] + acc_a[j] * cc for j in range(SEG)]
    return jnp.concatenate(hs, axis=0), new_carry


def _time_consts():
    r = lax.broadcasted_iota(jnp.int32, (CHUNK, CHUNK), 0)
    c = lax.broadcasted_iota(jnp.int32, (CHUNK, CHUNK), 1)
    tr = (r % SUBLANES) * SEG + r // SUBLANES
    tc = (c % SUBLANES) * SEG + c // SUBLANES
    return tr, tc


def _expand_mat(first_row):
    r = lax.broadcasted_iota(jnp.int32, (LANES, D_GROUP), 0)
    c = lax.broadcasted_iota(jnp.int32, (LANES, D_GROUP), 1)
    return jnp.where(r == first_row + c // SSD_HEAD_DIM, 1.0, 0.0)


def _ret_gamma_log(h):
    return math.log1p(-(2.0 ** (-5.0 - h)))


def _neg_exp_alog(alog_ref):
    lane = lax.broadcasted_iota(jnp.int32, (1, LANES), 1)
    return jnp.where(lane < 2 * SSD_HEADS, -jnp.exp(alog_ref[...]), 0.0)


def _ret_consts(rc_ref, tr, tc):
    dist = jnp.abs(tr - tc).astype(F32)
    pos = tr.astype(F32)
    for hd in range(RET_HEADS):
        lg = _ret_gamma_log(hd)
        rc_ref[hd] = jnp.exp(lg * dist)
        rc_ref[RET_HEADS + hd] = jnp.exp(lg * (pos + 1.0))
        rc_ref[2 * RET_HEADS + hd] = jnp.exp(lg * (CHUNK - 1.0 - pos))
        rc_ref[3 * RET_HEADS + hd] = jnp.exp(lg * (CHUNK - pos))
        rc_ref[4 * RET_HEADS + hd] = jnp.exp(lg * pos)


def _retention_scores(act_ref, rc_ref, rows):
    zeros_h = jnp.zeros((CHUNK, RET_HEAD_DIM), BF16)
    scs = []
    kvs = []
    for hp in range(RET_HEADS // 2):
        lo = 2 * hp * RET_HEAD_DIM
        q2 = act_ref[rows, Q_OFF + lo:Q_OFF + lo + 2 * RET_HEAD_DIM]
        k2 = act_ref[rows, K_OFF + lo:K_OFF + lo + 2 * RET_HEAD_DIM]
        v2 = act_ref[rows, V_OFF + lo:V_OFF + lo + 2 * RET_HEAD_DIM]
        ka, kb = k2[:, 0:RET_HEAD_DIM], k2[:, RET_HEAD_DIM:]
        kdiag = jnp.concatenate([jnp.concatenate([ka, zeros_h], axis=1),
                                 jnp.concatenate([zeros_h, kb], axis=1)], axis=0)
        sc = lax.dot_general(q2, kdiag, _NT, preferred_element_type=F32)
        dec = jnp.concatenate([rc_ref[2 * hp], rc_ref[2 * hp + 1]], axis=1)
        scs.append((sc * dec).astype(BF16))
        for j, kk in enumerate((ka, kb)):
            kd = (kk.astype(F32) * rc_ref[2 * RET_HEADS + 2 * hp + j]).astype(BF16)
            kvs.append(lax.dot_general(kd, v2[:, j * RET_HEAD_DIM:(j + 1) * RET_HEAD_DIM], _TN,
                                       preferred_element_type=F32))
    return scs, kvs


def _retention_values(act_ref, rows, scs):
    zeros_h = jnp.zeros((CHUNK, RET_HEAD_DIM), BF16)
    ys = []
    for hp in range(RET_HEADS // 2):
        lo = 2 * hp * RET_HEAD_DIM
        v2 = act_ref[rows, V_OFF + lo:V_OFF + lo + 2 * RET_HEAD_DIM]
        vdiag = jnp.concatenate([jnp.concatenate([v2[:, 0:RET_HEAD_DIM], zeros_h], axis=1),
                                 jnp.concatenate([zeros_h, v2[:, RET_HEAD_DIM:]], axis=1)], axis=0)
        ys.append(_dot(scs[hp], vdiag))
    return ys


def _mixf_body(cv_ref, act_ref, dts_ref, wg_ref, bg_ref, lam_ref,
               alog_ref, dsk_ref, part_ref,
               hc_ref, sf_ref, rf_ref, rc_ref):
    i = pl.program_id(1)
    tb = act_ref.shape[0]
    tr, tc = _time_consts()

    @pl.when(i == 0)
    def _():
        hc_ref[...] = jnp.zeros(hc_ref.shape, F32)
        sf_ref[...] = jnp.zeros(sf_ref.shape, F32)
        rf_ref[...] = jnp.zeros(rf_ref.shape, F32)

    @pl.when((i == 0) & (pl.program_id(0) == 0))
    def _():
        _ret_consts(rc_ref, tr, tc)

    nck = tb // CHUNK
    rows = [slice(c * CHUNK, (c + 1) * CHUNK) for c in range(nck)]
    lanes = rows

    ret_sc = []
    ret_kv = []
    for c in range(nck):
        scs, kvs = _retention_scores(act_ref, rc_ref, rows[c])
        ret_sc.append(scs)
        ret_kv.append(kvs)

    xc = cv_ref[:, 0:D_GROUP].astype(F32)
    r_pre, i_pre = _lru_gate_matmul(xc, wg_ref)
    sp = _lru_rate(lam_ref)

    lower = tr > tc
    diag = tr == tc
    ltri = jnp.where(tr >= tc, 1.0, 0.0)
    ltri3 = jnp.concatenate([ltri, ltri, ltri], axis=1).astype(BF16)
    utri = jnp.where(tr <= tc, 1.0, 0.0)
    expand = _expand_mat(0)
    a_row = _neg_exp_alog(alog_ref)
    last = CHUNK - 1
    lane = lax.broadcasted_iota(jnp.int32, (CHUNK, LANES), 1)
    first_head = lane < SSD_HEAD_DIM

    dt_all = jnp.concatenate([dts_ref[r, :] for r in rows], axis=1)
    da_all = dt_all * jnp.concatenate([a_row * LOG2E] * nck, axis=1)
    cum_all = _cumsum_rows(ltri3, da_all)
    tot_all = cum_all[last:last + 1, :]
    rcum_all = tot_all - cum_all + da_all
    da_t = jnp.concatenate([da_all[:, l].T[0:BF16_SUBLANES, :] for l in lanes], axis=0)
    dt_t = jnp.concatenate([dt_all[:, l].T[0:BF16_SUBLANES, :] for l in lanes], axis=0)
    cum_t = _dot_hi(da_t, utri)
    rcum_t = cum_t[:, last:last + 1] - cum_t + da_t
    ldt = jnp.log2(dt_t)
    sub_f = cum_t - ldt
    sub_b = rcum_t - ldt
    e_rows = []
    for l in lanes:
        e_rows += [jnp.exp2(tot_all[:, l] - cum_all[:, l]) * dt_all[:, l], jnp.exp2(cum_all[:, l])]
    e2 = _dot(jnp.concatenate(e_rows, axis=0).astype(BF16), expand.astype(BF16))
    cdec = _dot_hi(jnp.concatenate([jnp.broadcast_to(jnp.exp2(tot_all[:, l]), (SUBLANES, LANES)) for l in lanes],
                                   axis=0), expand)

    cbs = []
    st_all = []
    for c in range(nck):
        xb = cv_ref[rows[c], D_GROUP:2 * D_GROUP]
        bm = cv_ref[rows[c], 2 * D_GROUP:2 * D_GROUP + LANES]
        cm = cv_ref[rows[c], 2 * D_GROUP + LANES:N_CONV]
        xd = xb * e2[2 * c * CHUNK:(2 * c + 1) * CHUNK, :].astype(BF16)
        cbs.append([lax.dot_general(cm[:, g * SSD_STATE:(g + 1) * SSD_STATE], bm[:, g * SSD_STATE:(g + 1) * SSD_STATE],
                                    _NT, preferred_element_type=F32) for g in range(SSD_GROUPS)])
        st_all.append([lax.dot_general(bm[:, g * SSD_STATE:(g + 1) * SSD_STATE], xd[:, g * 256:(g + 1) * 256],
                                       _TN, preferred_element_type=F32) for g in range(SSD_GROUPS)])

    ret_intra = [_retention_values(act_ref, rows[c], ret_sc[c]) for c in range(nck)]
    y_ssd = []
    scans = []
    for c in range(nck):
        xb = cv_ref[rows[c], D_GROUP:2 * D_GROUP]
        x = xb.astype(F32)
        cum = cum_all[:, lanes[c]]
        rcum = rcum_all[:, lanes[c]]
        ldiag = jnp.log2(dt_t[16 * c:16 * c + SSD_HEADS, :] + dt_t[16 * c + SSD_HEADS:16 * (c + 1), :])
        ys = []
        for g in range(SSD_GROUPS):
            for pp in range(SSD_HPG // 2):
                ws = []
                for hd in (g * SSD_HPG + 2 * pp, g * SSD_HPG + 2 * pp + 1):
                    hb = SSD_HEADS + hd
                    rf = 16 * c + hd
                    rb = 16 * c + hb
                    z = jnp.where(lower, cum[:, hd:hd + 1] - sub_f[rf:rf + 1, :],
                                  jnp.where(diag, ldiag[hd:hd + 1, :], rcum[:, hb:hb + 1] - sub_b[rb:rb + 1, :]))
                    ws.append((cbs[c][g] * jnp.exp2(z)).astype(BF16))
                p0 = (g * SSD_HPG + 2 * pp) * SSD_HEAD_DIM
                xp = xb[:, p0:p0 + LANES]
                zero = jnp.zeros_like(xp)
                xdiag = jnp.concatenate([jnp.where(first_head, xp, zero), jnp.where(first_head, zero, xp)], axis=0)
                ys.append(_dot(jnp.concatenate(ws, axis=1), xdiag))
        y_ssd.append(jnp.concatenate(ys, axis=1) + dsk_ref[...] * x)
        a, u = _lru_coeffs(xc[rows[c], :], r_pre[rows[c], :], i_pre[rows[c], :], bg_ref, sp)
        scans.append(_seg_scan(a, u, reverse=False))

    h, hc_ref[...] = _lru_link(scans, hc_ref[...], reverse=False)
    part_ref[:, 0:D_GROUP] = h.astype(BF16)

    for c in range(nck):
        cm = cv_ref[rows[c], 2 * D_GROUP + LANES:N_CONV]
        ecum_x = e2[(2 * c + 1) * CHUNK:(2 * c + 2) * CHUNK, :]
        cdec_x = cdec[SUBLANES * c:SUBLANES * c + 1, :]
        yo = [_dot(cm[:, g * SSD_STATE:(g + 1) * SSD_STATE], sf_ref[g].astype(BF16)) for g in range(SSD_GROUPS)]
        inter = [_dot(act_ref[rows[c], Q_OFF + hd * RET_HEAD_DIM:Q_OFF + (hd + 1) * RET_HEAD_DIM],
                      rf_ref[hd].astype(BF16)) for hd in range(RET_HEADS)]
        for g in range(SSD_GROUPS):
            sf_ref[g] = sf_ref[g] * cdec_x[:, g * 256:(g + 1) * 256] + st_all[c][g]
        for hd in range(RET_HEADS):
            rf_ref[hd] = rf_ref[hd] * math.exp(_ret_gamma_log(hd) * CHUNK) + ret_kv[c][hd]
        part_ref[rows[c], D_GROUP:2 * D_GROUP] = (y_ssd[c] + jnp.concatenate(yo, axis=1) * ecum_x).astype(BF16)
        part_ref[rows[c], 2 * D_GROUP:3 * D_GROUP] = (
            jnp.concatenate(ret_intra[c], axis=1)
            + jnp.concatenate([inter[hd] * rc_ref[RET_HEADS + hd] for hd in range(RET_HEADS)], axis=1)).astype(BF16)


def _mix_fwd(cv, qkv, dts, l, wg, bg, lam, alog, dsk):
    b, s, _ = cv.shape
    tb = min(MIX_TOKENS, s)
    row = lambda bi, i: (bi, i, 0)
    return pl.pallas_call(
        _mixf_body,
        out_shape=jax.ShapeDtypeStruct((b, s, N_PART), BF16),
        grid=(b, s // tb),
        in_specs=[
            pl.BlockSpec((None, tb, N_CONV), row),
            pl.BlockSpec((None, tb, N_QKV), row),
            pl.BlockSpec((None, tb, LANES), row),
            _layer_spec((D_GROUP // LANES, LANES, 2 * LANES), l),
            _layer_spec((1, 2 * D_GROUP), l),
            _layer_spec((1, D_GROUP), l),
            _layer_spec((1, LANES), l),
            _layer_spec((1, D_GROUP), l),
        ],
        out_specs=pl.BlockSpec((None, tb, N_PART), row),
        scratch_shapes=[
            pltpu.VMEM((1, D_GROUP), F32),
            pltpu.VMEM((SSD_GROUPS, SSD_STATE, SSD_HPG * SSD_HEAD_DIM), F32),
            pltpu.VMEM((RET_HEADS, RET_HEAD_DIM, RET_HEAD_DIM), F32),
            pltpu.VMEM((5 * RET_HEADS, CHUNK, CHUNK), F32),
        ],
        compiler_params=pltpu.CompilerParams(
            dimension_semantics=("arbitrary", "arbitrary"), vmem_limit_bytes=VMEM_LIMIT),
        name="mix_fwd",
    )(cv, qkv, dts, wg, bg, lam, alog, dsk)


def _mixb_body(x_ref, act_ref, cv_ref, gate_ref, dts_ref, part_ref,
               wg_ref, bg_ref, lam_ref, alog_ref,
               snw_ref, rnw_ref, wo_ref, o_ref,
               hc_ref, sb_ref, rb_ref, rc_ref):
    i = pl.program_id(1)
    tb = act_ref.shape[0]
    tr, tc = _time_consts()
    nck = tb // CHUNK
    rows = [slice(c * CHUNK, (c + 1) * CHUNK) for c in range(nck)]
    lanes = rows

    @pl.when(i == 0)
    def _():
        hc_ref[...] = jnp.zeros(hc_ref.shape, F32)
        sb_ref[...] = jnp.zeros(sb_ref.shape, F32)
        rb_ref[...] = jnp.zeros(rb_ref.shape, F32)

    @pl.when((i == 0) & (pl.program_id(0) == 0))
    def _():
        _ret_consts(rc_ref, tr, tc)

    xc = cv_ref[:, 0:D_GROUP].astype(F32)
    r_pre, i_pre = _lru_gate_matmul(xc, wg_ref)
    kv_all = []
    for c in range(nck):
        kvs = []
        for hd in range(RET_HEADS):
            k = act_ref[rows[c], K_OFF + hd * RET_HEAD_DIM:K_OFF + (hd + 1) * RET_HEAD_DIM]
            v = act_ref[rows[c], V_OFF + hd * RET_HEAD_DIM:V_OFF + (hd + 1) * RET_HEAD_DIM]
            kd = (k.astype(F32) * rc_ref[4 * RET_HEADS + hd]).astype(BF16)
            kvs.append(lax.dot_general(kd, v, _TN, preferred_element_type=F32))
        kv_all.append(kvs)
    ltri = jnp.where(tr >= tc, 1.0, 0.0)
    ltri3 = jnp.concatenate([ltri, ltri, ltri], axis=1).astype(BF16)
    expand = _expand_mat(SSD_HEADS)
    a_row = _neg_exp_alog(alog_ref)
    last = CHUNK - 1
    dt_all = jnp.concatenate([dts_ref[r, :] for r in rows], axis=1)
    da_all = dt_all * jnp.concatenate([a_row * LOG2E] * nck, axis=1)
    cum_all = _cumsum_rows(ltri3, da_all)
    tot_all = cum_all[last:last + 1, :]
    rcum_all = tot_all - cum_all + da_all
    e_rows = []
    for l in lanes:
        e_rows += [jnp.exp2(tot_all[:, l] - rcum_all[:, l]) * dt_all[:, l], jnp.exp2(rcum_all[:, l])]
    e2 = _dot(jnp.concatenate(e_rows, axis=0).astype(BF16), expand.astype(BF16))
    cdec = _dot_hi(jnp.concatenate([jnp.broadcast_to(jnp.exp2(tot_all[:, l]), (SUBLANES, LANES)) for l in lanes],
                                   axis=0), expand)
    st_all = []
    for c in range(nck):
        xb = cv_ref[rows[c], D_GROUP:2 * D_GROUP]
        bm = cv_ref[rows[c], 2 * D_GROUP:2 * D_GROUP + LANES]
        xd = xb * e2[2 * c * CHUNK:(2 * c + 1) * CHUNK, :].astype(BF16)
        st_all.append([lax.dot_general(bm[:, g * SSD_STATE:(g + 1) * SSD_STATE], xd[:, g * 256:(g + 1) * 256],
                                       _TN, preferred_element_type=F32) for g in range(SSD_GROUPS)])

    sp = _lru_rate(lam_ref)
    scans = [None] * nck
    y_ret = [None] * nck
    y_ssd = [None] * nck
    for c in range(nck - 1, -1, -1):
        cm = cv_ref[rows[c], 2 * D_GROUP + LANES:N_CONV]
        yo = [_dot(cm[:, g * SSD_STATE:(g + 1) * SSD_STATE], sb_ref[g].astype(BF16)) for g in range(SSD_GROUPS)]
        inter = [_dot(act_ref[rows[c], Q_OFF + hd * RET_HEAD_DIM:Q_OFF + (hd + 1) * RET_HEAD_DIM],
                      rb_ref[hd].astype(BF16)) for hd in range(RET_HEADS)]
        cdec_x = cdec[SUBLANES * c:SUBLANES * c + 1, :]
        for g in range(SSD_GROUPS):
            sb_ref[g] = sb_ref[g] * cdec_x[:, g * 256:(g + 1) * 256] + st_all[c][g]
        for hd in range(RET_HEADS):
            rb_ref[hd] = rb_ref[hd] * math.exp(_ret_gamma_log(hd) * CHUNK) + kv_all[c][hd]

        a, u = _lru_coeffs(xc[rows[c], :], r_pre[rows[c], :], i_pre[rows[c], :], bg_ref, sp)
        scans[c] = _seg_scan(a, u, reverse=True)

        ecum_x = e2[(2 * c + 1) * CHUNK:(2 * c + 2) * CHUNK, :]
        ys = part_ref[rows[c], D_GROUP:2 * D_GROUP].astype(F32) + jnp.concatenate(yo, axis=1) * ecum_x
        ys = ys * gate_ref[rows[c], D_GROUP:2 * D_GROUP].astype(F32)
        ys = ys * lax.rsqrt(jnp.mean(ys * ys, axis=-1, keepdims=True) + EPS) * snw_ref[...]
        y_ssd[c] = ys.astype(BF16)
        yr = []
        for hd in range(RET_HEADS):
            cs = slice(2 * D_GROUP + hd * RET_HEAD_DIM, 2 * D_GROUP + (hd + 1) * RET_HEAD_DIM)
            y = part_ref[rows[c], cs].astype(F32) + inter[hd] * rc_ref[3 * RET_HEADS + hd]
            mu = jnp.mean(y, axis=-1, keepdims=True)
            yc = y - mu
            var = jnp.mean(yc * yc, axis=-1, keepdims=True)
            ws = slice(hd * RET_HEAD_DIM, (hd + 1) * RET_HEAD_DIM)
            yr.append(yc * lax.rsqrt(var + EPS) * rnw_ref[:, ws] * gate_ref[rows[c], cs].astype(F32))
        y_ret[c] = jnp.concatenate(yr, axis=1).astype(BF16)

    h, hc_ref[...] = _lru_link(scans, hc_ref[...], reverse=True)
    y_lru = (part_ref[:, 0:D_GROUP].astype(F32) + h) * gate_ref[:, 0:D_GROUP].astype(F32)
    y = jnp.concatenate([y_lru.astype(BF16), jnp.concatenate(y_ssd, axis=0), jnp.concatenate(y_ret, axis=0)], axis=1)
    o_ref[...] = x_ref[...] + _dot(y, wo_ref[...])


def _mix_bwd(x, qkv, cv, gate, dts, part, l, wg, bg, lam, alog, snw, rnw, wo):
    b, s, _ = qkv.shape
    tb = min(MIX_TOKENS, s)
    nb = s // tb
    row = lambda bi, i: (bi, nb - 1 - i, 0)
    return pl.pallas_call(
        _mixb_body,
        out_shape=jax.ShapeDtypeStruct((b, s, D_MODEL), F32),
        grid=(b, nb),
        in_specs=[
            pl.BlockSpec((None, tb, D_MODEL), row),
            pl.BlockSpec((None, tb, N_QKV), row),
            pl.BlockSpec((None, tb, N_CONV), row),
            pl.BlockSpec((None, tb, N_GATE), row),
            pl.BlockSpec((None, tb, LANES), row),
            pl.BlockSpec((None, tb, N_PART), row),
            _layer_spec((D_GROUP // LANES, LANES, 2 * LANES), l),
            _layer_spec((1, 2 * D_GROUP), l),
            _layer_spec((1, D_GROUP), l),
            _layer_spec((1, LANES), l),
            _layer_spec((1, D_GROUP), l),
            _layer_spec((1, D_GROUP), l),
            _layer_spec((N_PART, D_MODEL), l),
        ],
        out_specs=pl.BlockSpec((None, tb, D_MODEL), row),
        scratch_shapes=[
            pltpu.VMEM((1, D_GROUP), F32),
            pltpu.VMEM((SSD_GROUPS, SSD_STATE, SSD_HPG * SSD_HEAD_DIM), F32),
            pltpu.VMEM((RET_HEADS, RET_HEAD_DIM, RET_HEAD_DIM), F32),
            pltpu.VMEM((5 * RET_HEADS, CHUNK, CHUNK), F32),
        ],
        compiler_params=pltpu.CompilerParams(
            dimension_semantics=("arbitrary", "arbitrary"), vmem_limit_bytes=VMEM_LIMIT),
        name="mix_bwd",
    )(x, qkv, cv, gate, dts, part, wg, bg, lam, alog, snw, rnw, wo)


def _to_chunk_order(x):
    b, s, d = x.shape
    return x.reshape(b, s // CHUNK, SUBLANES, SEG, d).swapaxes(2, 3).reshape(b, s, d)


def _from_chunk_order(x):
    b, s, d = x.shape
    return x.reshape(b, s // CHUNK, SEG, SUBLANES, d).swapaxes(2, 3).reshape(b, s, d)


def _pair_blocks(w):
    d, two, nb, bw, _ = w.shape
    w = w.reshape(d, two, nb // 2, 2, bw, bw)
    z = jnp.zeros_like(w[:, :, :, 0])
    top = jnp.concatenate([w[:, :, :, 0], z], axis=-1)
    bot = jnp.concatenate([z, w[:, :, :, 1]], axis=-1)
    return jnp.concatenate([top, bot], axis=-2)


def _prep(p):
    w_in = p['w_in']
    o = [0, 512, 1024, 1536, 2304, 2320, 2832, 3344, 3856, 4368]
    lru_x, lru_gate, ssd_z, xbc, dtc, q, k, v, g = [w_in[:, :, o[j]:o[j + 1]] for j in range(9)]
    depth = w_in.shape[0]
    row = lambda a: a.reshape(depth, 1, -1)
    pad = lambda a: jnp.pad(a.reshape(depth, -1), ((0, 0), (0, LANES - a.shape[1] * a.shape[2])))[:, None, :]
    gate_w = jnp.concatenate([_pair_blocks(p['lru_w_a']), _pair_blocks(p['lru_w_i'])], axis=-1).astype(BF16)
    gate_b = jnp.concatenate([p['lru_b_a'], p['lru_b_i']], axis=-1)[:, :, None, :]
    return dict(
        ffn1_nw=row(p['ffn1_norm']), ffn1_wgu=p['ffn1_w_gu'].astype(BF16), ffn1_wd=p['ffn1_w_down'].astype(BF16),
        ffn2_nw=row(p['ffn2_norm']), ffn2_wgu=p['ffn2_w_gu'].astype(BF16), ffn2_wd=p['ffn2_w_down'].astype(BF16),
        mix_nw=row(p['mix_norm']),
        wc=jnp.concatenate([lru_x, xbc], axis=2).astype(BF16),
        wq=jnp.concatenate([q, k, v], axis=2).astype(BF16),
        wg=jnp.concatenate([lru_gate, ssd_z, g], axis=2).astype(BF16),
        wdt=jnp.pad(dtc, ((0, 0), (0, 0), (0, LANES - dtc.shape[2]))).astype(BF16),
        cw=jnp.concatenate([p['lru_conv_w'], p['ssd_conv_w']], axis=2),
        cb=row(jnp.concatenate([p['lru_conv_b'], p['ssd_conv_b']], axis=1)),
        gate_w=[gate_w[:, d] for d in range(2)], gate_b=[gate_b[:, d] for d in range(2)],
        lam=[p['lru_lam'][:, d][:, None, :] for d in range(2)],
        dtb=pad(p['ssd_dt_bias']), alog=pad(p['ssd_a_log']),
        dsk=row(jnp.repeat(p['ssd_d'], SSD_HEAD_DIM, axis=1)),
        snw=row(p['ssd_norm']), rnw=row(p['ret_norm']),
        wo=p['w_out'].astype(BF16),
    )


def _rope_tables(s):
    d = RET_HEAD_DIM
    inv_freq = 1.0 / (ROPE_BASE ** (jnp.arange(0, d, 2, dtype=F32) / d))
    r = jnp.arange(s, dtype=jnp.int32)
    local = r % CHUNK
    pos = (r // CHUNK) * CHUNK + (local % SUBLANES) * SEG + local // SUBLANES
    ang = pos.astype(F32)[:, None] * inv_freq[None, :]
    cos = jnp.cos(ang)
    sin = jnp.sin(ang)
    return jnp.concatenate([cos, cos], axis=1), jnp.concatenate([-sin, sin], axis=1)


def _trunk(x, pp, depth, final_nw):
    _, s, _ = x.shape
    c2, s2 = _rope_tables(s)
    x = _to_chunk_order(x)
    for l in range(depth):
        x = _ffn(x, l, pp['ffn1_nw'], pp['ffn1_wgu'], pp['ffn1_wd'])
        cv, qkv, gate, dts = _proj(x, l, pp['mix_nw'], pp['wc'], pp['wq'], pp['wg'], pp['wdt'], c2, s2, pp['cw'], pp['cb'],
                                   pp['dtb'])
        part = _mix_fwd(cv, qkv, dts, l, pp['gate_w'][0], pp['gate_b'][0], pp['lam'][0], pp['alog'], pp['dsk'])
        x = _mix_bwd(x, qkv, cv, gate, dts, part, l, pp['gate_w'][1], pp['gate_b'][1], pp['lam'][1],
                     pp['alog'], pp['snw'], pp['rnw'], pp['wo'])
        x = _ffn(x, l, pp['ffn2_nw'], pp['ffn2_wgu'], pp['ffn2_wd'], final_nw if l == depth - 1 else None)
    return _from_chunk_order(x)


def kernel(x_prompt, x_sample, ffn1_norm, ffn1_w_gu, ffn1_w_down, mix_norm, w_in, lru_conv_w, lru_conv_b, lru_w_a, lru_b_a, lru_w_i, lru_b_i, lru_lam, ssd_conv_w, ssd_conv_b, ssd_dt_bias, ssd_a_log, ssd_d, ssd_norm, ret_norm, w_out, ffn2_norm, ffn2_w_gu, ffn2_w_down, final_norm):
    p = dict(ffn1_norm=ffn1_norm, ffn1_w_gu=ffn1_w_gu, ffn1_w_down=ffn1_w_down, mix_norm=mix_norm, w_in=w_in,
             lru_conv_w=lru_conv_w, lru_conv_b=lru_conv_b, lru_w_a=lru_w_a, lru_b_a=lru_b_a, lru_w_i=lru_w_i,
             lru_b_i=lru_b_i, lru_lam=lru_lam, ssd_conv_w=ssd_conv_w, ssd_conv_b=ssd_conv_b, ssd_dt_bias=ssd_dt_bias,
             ssd_a_log=ssd_a_log, ssd_d=ssd_d, ssd_norm=ssd_norm, ret_norm=ret_norm, w_out=w_out,
             ffn2_norm=ffn2_norm, ffn2_w_gu=ffn2_w_gu, ffn2_w_down=ffn2_w_down)
    pp = _prep(p)
    depth = ffn1_norm.shape[0]
    final_nw = final_norm[None, :]
    return (_trunk(x_prompt, pp, depth, final_nw), _trunk(x_sample, pp, depth, final_nw))
```
